```python
import math
import jax, jax.numpy as jnp
from jax import lax
import numpy as np

D_MODEL = 4096
BATCH = 2
SEQ = 8192
DEPTH = 2

HEAD_DIM = 128
N_TOTAL_HEADS = D_MODEL // HEAD_DIM
GRID_W = 64
N_MEM = 256
A_Q_HEADS = 3 * N_TOTAL_HEADS // 8
A_KV_HEADS = A_Q_HEADS // 3
A_WINDOW = 128
A_BLOCK = 128
B_HEADS = N_TOTAL_HEADS // 4
B_WIN_ROWS = 8
B_WIN_COLS = 16
B_QCOLS = 16
B_KCOLS = 32
C_HEADS = N_TOTAL_HEADS - A_Q_HEADS - B_HEADS
C_Q_RANK = 1024
C_KV_RANK = 512
C_NOPE = 128
C_ROPE = 64
C_V = HEAD_DIM
C_BLOCK = 128
ROPE_BASE = 10000.0
X_HEADS = 4
X_DIM = X_HEADS * HEAD_DIM
D_FF = -(-8 * D_MODEL // 768) * 256

IN_WIDTHS = (A_Q_HEADS * HEAD_DIM, A_KV_HEADS * HEAD_DIM, A_KV_HEADS * HEAD_DIM,
             B_HEADS * HEAD_DIM, B_HEADS * HEAD_DIM, B_HEADS * HEAD_DIM,
             C_Q_RANK, C_KV_RANK, C_ROPE)
D_IN = sum(IN_WIDTHS)
MIX_W = (A_Q_HEADS + B_HEADS + C_HEADS) * HEAD_DIM
NORM_EPS = 1e-6
NEG_INF = -1e30

kernel_name = 'hybrid_parallel_heads_encoder'


def rms_norm(x, g):
    xf = x.astype(jnp.float32)
    y = xf * lax.rsqrt(jnp.mean(xf * xf, axis=-1, keepdims=True) + NORM_EPS)
    return (y * g.astype(jnp.float32)).astype(x.dtype)


def alibi_slopes(n):
    return 2.0 ** (-8.0 * jnp.arange(1, n + 1, dtype=jnp.float32) / n)


def rope_tables(s):
    inv = 1.0 / (ROPE_BASE ** (jnp.arange(0, C_ROPE, 2, dtype=jnp.float32) / C_ROPE))
    ang = jnp.arange(s, dtype=jnp.float32)[:, None] * inv[None, :]
    return jnp.cos(ang), jnp.sin(ang)


def apply_rope(x, cos, sin):
    x1, x2 = jnp.split(x.astype(jnp.float32), 2, axis=-1)
    c = cos[None, :, None, :]
    sn = sin[None, :, None, :]
    return jnp.concatenate([x1 * c - x2 * sn, x1 * sn + x2 * c], axis=-1).astype(x.dtype)


def windowed_gqa_alibi(q, k, v, sink):
    b, s, hq, d = q.shape
    hkv = k.shape[2]
    rep = hq // hkv
    nb = s // A_BLOCK
    qb = q.reshape(b, nb, A_BLOCK, hkv, rep, d)

    def band(t):
        tp = jnp.pad(t, ((0, 0), (A_BLOCK, A_BLOCK), (0, 0), (0, 0))).reshape(b, nb + 2, A_BLOCK, hkv, d)
        return jnp.concatenate([tp[:, :-2], tp[:, 1:-1], tp[:, 2:]], axis=2)

    kb, vb = band(k), band(v)
    scores = jnp.einsum('bnqgrd,bnkgd->bngrqk', qb, kb).astype(jnp.float32) * (1.0 / math.sqrt(d))
    qi = jnp.arange(A_BLOCK)[:, None]
    kj = jnp.arange(3 * A_BLOCK)[None, :]
    rel = kj - A_BLOCK - qi
    kpos = jnp.arange(nb)[:, None, None] * A_BLOCK - A_BLOCK + kj[None]
    valid = (jnp.abs(rel) <= A_WINDOW)[None] & (kpos >= 0) & (kpos < s)
    slopes = alibi_slopes(hq).reshape(hkv, rep)
    scores = scores - slopes[None, None, :, :, None, None] * jnp.abs(rel).astype(jnp.float32)
    scores = jnp.where(valid[None, :, None, None], scores, NEG_INF)
    sink_l = sink.astype(jnp.float32).reshape(hkv, rep)[None, None, :, :, None, None]
    m = jnp.maximum(jnp.max(scores, axis=-1, keepdims=True), sink_l)
    p = jnp.exp(scores - m)
    denom = jnp.sum(p, axis=-1, keepdims=True) + jnp.exp(sink_l - m)
    out = jnp.einsum('bngrqk,bnkgd->bnqgrd', (p / denom).astype(v.dtype), vb)
    return out.reshape(b, s, hq * d)


def neighbourhood_attn_2d(q, k, v, rpb):
    b, s, h, d = q.shape
    rows = s // GRID_W
    wr = min(B_WIN_ROWS, rows)
    qg = q.reshape(b, rows, GRID_W, h, d)
    kg = k.reshape(b, rows, GRID_W, h, d)
    vg = v.reshape(b, rows, GRID_W, h, d)
    nj = GRID_W // B_QCOLS
    qcol = np.arange(GRID_W).reshape(nj, B_QCOLS)
    kstart = np.clip(np.arange(nj) * B_QCOLS - B_WIN_COLS // 2, 0, GRID_W - B_KCOLS)
    kcol = kstart[:, None] + np.arange(B_KCOLS)[None, :]
    cstart = np.clip(qcol - B_WIN_COLS // 2, 0, GRID_W - B_WIN_COLS)
    col_mask = (kcol[:, None, :] >= cstart[..., None]) & (kcol[:, None, :] < cstart[..., None] + B_WIN_COLS)
    dc_idx = np.clip(kcol[:, None, :] - qcol[..., None] + B_WIN_COLS - 1, 0, 2 * B_WIN_COLS - 2)
    rpb_c = rpb.astype(jnp.float32)[:, :, dc_idx]
    col_mask = jnp.asarray(col_mask)
    scale = 1.0 / math.sqrt(d)

    def row_block(r):
        rs = jnp.clip(r - wr // 2, 0, rows - wr)
        qr = lax.dynamic_index_in_dim(qg, r, axis=1, keepdims=False).reshape(b, nj, B_QCOLS, h, d)
        kr = lax.dynamic_slice_in_dim(kg, rs, wr, axis=1)[:, :, kcol]
        vr = lax.dynamic_slice_in_dim(vg, rs, wr, axis=1)[:, :, kcol]
        sc = jnp.einsum('bjqhd,bwjchd->bhjqwc', qr, kr).astype(jnp.float32) * scale
        dr_idx = rs + jnp.arange(wr) - r + B_WIN_ROWS - 1
        bias = jnp.take(rpb_c, dr_idx, axis=1)
        sc = sc + bias.transpose(0, 2, 3, 1, 4)[None]
        sc = jnp.where(col_mask[None, None, :, :, None, :], sc, NEG_INF)
        p = jax.nn.softmax(sc.reshape(b, h, nj, B_QCOLS, wr * B_KCOLS), axis=-1)
        p = p.reshape(b, h, nj, B_QCOLS, wr, B_KCOLS).astype(vr.dtype)
        out = jnp.einsum('bhjqwc,bwjchd->bjqhd', p, vr)
        return out.reshape(b, GRID_W, h * d)

    out = lax.map(row_block, jnp.arange(rows))
    return out.transpose(1, 0, 2, 3).reshape(b, s, h * d)


def mla(c_q, c_kv, k_rope_in, q_norm, w_q_b, kv_norm, w_kv_b, cos, sin):
    b, s, _ = c_q.shape
    q = (rms_norm(c_q, q_norm) @ w_q_b).reshape(b, s, C_HEADS, C_NOPE + C_ROPE)
    q_nope, q_rope = q[..., :C_NOPE], apply_rope(q[..., C_NOPE:], cos, sin)
    kv = (rms_norm(c_kv, kv_norm) @ w_kv_b).reshape(b, s, C_HEADS, C_NOPE + C_V)
    k_nope, vv = kv[..., :C_NOPE], kv[..., C_NOPE:]
    k_rope = apply_rope(k_rope_in[:, :, None, :], cos, sin)[:, :, 0]
    scale = 1.0 / math.sqrt(C_NOPE + C_ROPE)
    nb = s // C_BLOCK
    qn_b = q_nope.reshape(b, nb, C_BLOCK, C_HEADS, C_NOPE).transpose(1, 0, 2, 3, 4)
    qr_b = q_rope.reshape(b, nb, C_BLOCK, C_HEADS, C_ROPE).transpose(1, 0, 2, 3, 4)

    def blk(args):
        qn, qr = args
        sc = (jnp.einsum('bqhd,bkhd->bhqk', qn, k_nope).astype(jnp.float32)
              + jnp.einsum('bqhr,bkr->bhqk', qr, k_rope).astype(jnp.float32)) * scale
        p = jax.nn.softmax(sc, axis=-1).astype(vv.dtype)
        return jnp.einsum('bhqk,bkhd->bqhd', p, vv)

    out = lax.map(blk, (qn_b, qr_b))
    return out.transpose(1, 0, 2, 3, 4).reshape(b, s, C_HEADS * C_V)


def memory_xattn(h, mem_n, w_q, w_k, w_v, w_o):
    b, s, _ = h.shape
    m = mem_n.shape[1]
    q = (h @ w_q).reshape(b, s, X_HEADS, HEAD_DIM)
    k = (mem_n @ w_k).reshape(b, m, X_HEADS, HEAD_DIM)
    v = (mem_n @ w_v).reshape(b, m, X_HEADS, HEAD_DIM)
    sc = jnp.einsum('bshd,bmhd->bhsm', q, k).astype(jnp.float32) * (1.0 / math.sqrt(HEAD_DIM))
    p = jax.nn.softmax(sc, axis=-1).astype(v.dtype)
    o = jnp.einsum('bhsm,bmhd->bshd', p, v).reshape(b, s, X_DIM)
    return o @ w_o


def swiglu(h, w_gate, w_up, w_down):
    return (jax.nn.silu(h @ w_gate) * (h @ w_up)) @ w_down


def _w(key, shape, fan_in):
    return jax.random.normal(key, shape, jnp.float32) * (fan_in ** -0.5)


def _gain(key, shape):
    return 1.0 + 0.02 * jax.random.normal(key, shape, jnp.float32)


def setup_inputs(seed: int = 0) -> dict:
    key = jax.random.key(seed)
    ks = jax.random.split(key, 24)
    L = DEPTH
    return {
        'x': jax.random.normal(ks[0], (BATCH, SEQ, D_MODEL), jnp.float32),
        'mem': jax.random.normal(ks[1], (BATCH, N_MEM, D_MODEL), jnp.float32),
        'ln_mix': _gain(ks[2], (L, D_MODEL)),
        'w_in': _w(ks[3], (L, D_MODEL, D_IN), D_MODEL),
        'a_sink': 0.5 * jax.random.normal(ks[4], (L, A_Q_HEADS), jnp.float32),
        'b_rpb': 0.1 * jax.random.normal(ks[5], (L, B_HEADS, 2 * B_WIN_ROWS - 1, 2 * B_WIN_COLS - 1), jnp.float32),
        'c_q_norm': _gain(ks[6], (L, C_Q_RANK)),
        'c_w_q_b': _w(ks[7], (L, C_Q_RANK, C_HEADS * (C_NOPE + C_ROPE)), C_Q_RANK),
        'c_kv_norm': _gain(ks[8], (L, C_KV_RANK)),
        'c_w_kv_b': _w(ks[9], (L, C_KV_RANK, C_HEADS * (C_NOPE + C_V)), C_KV_RANK),
        'w_out': _w(ks[10], (L, MIX_W, D_MODEL), MIX_W),
        'ln_xattn': _gain(ks[11], (L, D_MODEL)),
        'ln_mem': _gain(ks[12], (L, D_MODEL)),
        'x_w_q': _w(ks[13], (L, D_MODEL, X_DIM), D_MODEL),
        'x_w_k': _w(ks[14], (L, D_MODEL, X_DIM), D_MODEL),
        'x_w_v': _w(ks[15], (L, D_MODEL, X_DIM), D_MODEL),
        'x_w_o': _w(ks[16], (L, X_DIM, D_MODEL), X_DIM),
        'ln_ffn': _gain(ks[17], (L, D_MODEL)),
        'w_gate': _w(ks[18], (L, D_MODEL, D_FF), D_MODEL),
        'w_up': _w(ks[19], (L, D_MODEL, D_FF), D_MODEL),
        'w_down': _w(ks[20], (L, D_FF, D_MODEL), D_FF),
        'ln_final': _gain(ks[21], (D_MODEL,)),
    }


def reference(x, mem, ln_mix, w_in, a_sink, b_rpb, c_q_norm, c_w_q_b, c_kv_norm, c_w_kv_b, w_out,
              ln_xattn, ln_mem, x_w_q, x_w_k, x_w_v, x_w_o, ln_ffn, w_gate, w_up, w_down, ln_final):
    b, s, _ = x.shape
    cos, sin = rope_tables(s)
    split_at = np.cumsum(IN_WIDTHS)[:-1].tolist()
    for l in range(DEPTH):
        h = rms_norm(x, ln_mix[l])
        aq, ak, av, bq, bk, bv, cq, ckv, ckr = jnp.split(h @ w_in[l], split_at, axis=-1)
        ya = windowed_gqa_alibi(aq.reshape(b, s, A_Q_HEADS, HEAD_DIM),
                                ak.reshape(b, s, A_KV_HEADS, HEAD_DIM),
                                av.reshape(b, s, A_KV_HEADS, HEAD_DIM), a_sink[l])
        yb = neighbourhood_attn_2d(bq.reshape(b, s, B_HEADS, HEAD_DIM),
                                   bk.reshape(b, s, B_HEADS, HEAD_DIM),
                                   bv.reshape(b, s, B_HEADS, HEAD_DIM), b_rpb[l])
        yc = mla(cq, ckv, ckr, c_q_norm[l], c_w_q_b[l], c_kv_norm[l], c_w_kv_b[l], cos, sin)
        x = x + jnp.concatenate([ya, yb, yc], axis=-1) @ w_out[l]
        x = x + memory_xattn(rms_norm(x, ln_xattn[l]), rms_norm(mem, ln_mem[l]),
                             x_w_q[l], x_w_k[l], x_w_v[l], x_w_o[l])
        x = x + swiglu(rms_norm(x, ln_ffn[l]), w_gate[l], w_up[l], w_down[l])
    return rms_norm(x, ln_final)
```

```python
import functools
import math

import numpy as np
import jax
import jax.numpy as jnp
from jax import lax
from jax.experimental import pallas as pl
from jax.experimental.pallas import tpu as pltpu

F32 = jnp.float32
BF16 = jnp.bfloat16

HEAD_DIM = 128
GRID_W = 64
A_Q_HEADS = 12
A_KV_HEADS = 4
A_REP = A_Q_HEADS // A_KV_HEADS
A_WINDOW = 128
A_BLOCK = 128
B_HEADS = 8
B_WIN_ROWS = 8
B_WIN_COLS = 16
C_HEADS = 12
C_Q_RANK = 1024
C_KV_RANK = 512
C_NOPE = 128
C_ROPE = 64
C_V = 128
ROPE_BASE = 10000.0
X_HEADS = 4
NORM_EPS = 1e-6
NEG_INF = -1e30

VMEM_LIMIT_BYTES = 58 * 1024 * 1024
LANES = 128


def _params(*sem):
    return pltpu.CompilerParams(dimension_semantics=sem, vmem_limit_bytes=VMEM_LIMIT_BYTES)


def _tile(n, pref):
    if n <= pref:
        return n
    t = (pref // LANES) * LANES
    while t >= LANES:
        if n % t == 0:
            return t
        t -= LANES
    return n


def _rmsnorm_kernel(x_ref, g_ref, o_ref):
    x = x_ref[...].astype(F32)
    inv = lax.rsqrt(jnp.mean(x * x, axis=-1, keepdims=True) + NORM_EPS)
    o_ref[...] = (x * inv * g_ref[...]).astype(o_ref.dtype)


def rmsnorm(x, g, out_dtype, bm=256):
    m, d = x.shape
    bm = min(bm, m)
    return pl.pallas_call(
        _rmsnorm_kernel,
        grid=(m // bm,),
        in_specs=[pl.BlockSpec((bm, d), lambda i: (i, 0)), pl.BlockSpec((1, d), lambda i: (0, 0))],
        out_specs=pl.BlockSpec((bm, d), lambda i: (i, 0)),
        out_shape=jax.ShapeDtypeStruct((m, d), out_dtype),
        compiler_params=_params("parallel"),
        name="rmsnorm",
    )(x, g.reshape(1, d).astype(F32))


def _mm_kernel(*refs, n_a, nk, has_res):
    a_refs = refs[:n_a]
    w_refs = refs[n_a:2 * n_a]
    pos = 2 * n_a
    r_ref = refs[pos] if has_res else None
    pos += int(has_res)
    o_ref = refs[pos]
    acc_ref = refs[pos + 1] if nk > 1 else None

    part = jnp.dot(a_refs[0][...], w_refs[0][...], preferred_element_type=F32)
    for a_ref, w_ref in zip(a_refs[1:], w_refs[1:]):
        part = part + jnp.dot(a_ref[...], w_ref[...], preferred_element_type=F32)

    def finish(acc):
        if has_res:
            acc = r_ref[...] + acc
        o_ref[...] = acc.astype(o_ref.dtype)

    if nk == 1:
        finish(part)
    else:
        k = pl.program_id(2)

        @pl.when(k == 0)
        def _():
            acc_ref[...] = part

        @pl.when(jnp.logical_and(k > 0, k < nk - 1))
        def _():
            acc_ref[...] += part

        @pl.when(k == nk - 1)
        def _():
            finish(acc_ref[...] + part)


def matmul(a_list, w_list, out_dtype, res=None, bm=1024, bn=1024, bk=None, name="matmul"):
    if not isinstance(a_list, (list, tuple)):
        a_list, w_list = [a_list], [w_list]
    m = a_list[0].shape[0]
    n = w_list[0].shape[1]
    bm = _tile(m, bm)
    bn = _tile(n, bn)
    n_a = len(a_list)
    if bk is None or n_a > 1:
        nk = 1
    else:
        kdim = a_list[0].shape[1]
        assert kdim % bk == 0
        nk = kdim // bk
    in_specs = []
    for a in a_list:
        kd = a.shape[1] if nk == 1 else bk
        in_specs.append(pl.BlockSpec((bm, kd), lambda i, j, k: (i, k)))
    for w in w_list:
        kd = w.shape[0] if nk == 1 else bk
        in_specs.append(pl.BlockSpec((kd, bn), lambda i, j, k: (k, j)))
    args = list(a_list) + list(w_list)
    if res is not None:
        in_specs.append(pl.BlockSpec((bm, bn), lambda i, j, k: (i, j)))
        args.append(res)
    scratch = [pltpu.VMEM((bm, bn), F32)] if nk > 1 else []
    return pl.pallas_call(
        functools.partial(_mm_kernel, n_a=n_a, nk=nk, has_res=res is not None),
        grid=(m // bm, n // bn, nk),
        in_specs=in_specs,
        out_specs=pl.BlockSpec((bm, bn), lambda i, j, k: (i, j)),
        out_shape=jax.ShapeDtypeStruct((m, n), out_dtype),
        scratch_shapes=scratch,
        compiler_params=_params("parallel", "parallel", "arbitrary"),
        name=name,
    )(*args)


def _gate_up_kernel(h_ref, wg_ref, wu_ref, o_ref):
    h = h_ref[...]
    g = jnp.dot(h, wg_ref[...], preferred_element_type=F32)
    u = jnp.dot(h, wu_ref[...], preferred_element_type=F32)
    o_ref[...] = (jax.nn.silu(g) * u).astype(o_ref.dtype)


def gate_up(h, wg, wu, bm=1024, bn=512):
    m, d = h.shape
    n = wg.shape[1]
    bm = _tile(m, bm)
    bn = _tile(n, bn)
    return pl.pallas_call(
        _gate_up_kernel,
        grid=(m // bm, n // bn),
        in_specs=[
            pl.BlockSpec((bm, d), lambda i, j: (i, 0)),
            pl.BlockSpec((d, bn), lambda i, j: (0, j)),
            pl.BlockSpec((d, bn), lambda i, j: (0, j)),
        ],
        out_specs=pl.BlockSpec((bm, bn), lambda i, j: (i, j)),
        out_shape=jax.ShapeDtypeStruct((m, n), BF16),
        compiler_params=_params("parallel", "parallel"),
        name="gate_up",
    )(h, wg, wu)


def _mixer_a_kernel(slope_ref, sink_ref, q_ref, kp_ref, kc_ref, kn_ref, vp_ref, vc_ref, vn_ref, o_ref,
                    *, tq, seq):
    i = pl.program_id(1)
    g = pl.program_id(2)
    nsub = tq // A_BLOCK
    kcat = jnp.concatenate([kp_ref[...], kc_ref[...], kn_ref[...]], axis=0)
    vcat = jnp.concatenate([vp_ref[...], vc_ref[...], vn_ref[...]], axis=0)
    qi = lax.broadcasted_iota(jnp.int32, (A_BLOCK, 3 * A_BLOCK), 0)
    kj = lax.broadcasted_iota(jnp.int32, (A_BLOCK, 3 * A_BLOCK), 1)
    rel = kj - A_BLOCK - qi
    dist = jnp.abs(rel)
    dist_f = dist.astype(F32)
    in_window = dist <= A_WINDOW
    scale = 1.0 / math.sqrt(HEAD_DIM)
    for j in range(nsub):
        kband = kcat[j * A_BLOCK:(j + 3) * A_BLOCK]
        vband = vcat[j * A_BLOCK:(j + 3) * A_BLOCK]
        kpos = (i * nsub + j - 1) * A_BLOCK + kj
        valid = in_window & (kpos >= 0) & (kpos < seq)
        for r in range(A_REP):
            head = g * A_REP + r
            slope = slope_ref[head]
            sink = sink_ref[head]
            q = q_ref[j * A_BLOCK:(j + 1) * A_BLOCK, r * HEAD_DIM:(r + 1) * HEAD_DIM]
            s = lax.dot_general(q, kband, (((1,), (1,)), ((), ())), preferred_element_type=F32) * scale
            s = s - slope * dist_f
            s = jnp.where(valid, s, NEG_INF)
            mx = jnp.maximum(jnp.max(s, axis=-1, keepdims=True), sink)
            p = jnp.exp(s - mx)
            denom = jnp.sum(p, axis=-1, keepdims=True) + jnp.exp(sink - mx)
            o = jnp.dot((p / denom).astype(BF16), vband, preferred_element_type=F32)
            o_ref[j * A_BLOCK:(j + 1) * A_BLOCK, r * HEAD_DIM:(r + 1) * HEAD_DIM] = o.astype(o_ref.dtype)


def mixer_a(qkv, sink, batch, seq, tq=512):
    tq = min(tq, seq)
    nq = seq // tq
    nsub = tq // A_BLOCK
    nblk = seq // A_BLOCK
    kcol = A_Q_HEADS
    vcol = A_Q_HEADS + A_KV_HEADS
    slopes = 2.0 ** (-8.0 * jnp.arange(1, A_Q_HEADS + 1, dtype=F32) / A_Q_HEADS)

    def cur(col):
        return pl.BlockSpec((tq, HEAD_DIM), lambda b, i, g: (b * nq + i, col + g))

    def prev(col):
        return pl.BlockSpec((A_BLOCK, HEAD_DIM),
                            lambda b, i, g: (b * nblk + jnp.maximum(i * nsub - 1, 0), col + g))

    def nxt(col):
        return pl.BlockSpec((A_BLOCK, HEAD_DIM),
                            lambda b, i, g: (b * nblk + jnp.minimum((i + 1) * nsub, nblk - 1), col + g))

    smem = pl.BlockSpec(memory_space=pltpu.SMEM)
    return pl.pallas_call(
        functools.partial(_mixer_a_kernel, tq=tq, seq=seq),
        grid=(batch, nq, A_KV_HEADS),
        in_specs=[smem, smem,
                  pl.BlockSpec((tq, A_REP * HEAD_DIM), lambda b, i, g: (b * nq + i, g)),
                  prev(kcol), cur(kcol), nxt(kcol), prev(vcol), cur(vcol), nxt(vcol)],
        out_specs=pl.BlockSpec((tq, A_REP * HEAD_DIM), lambda b, i, g: (b * nq + i, g)),
        out_shape=jax.ShapeDtypeStruct((batch * seq, A_Q_HEADS * HEAD_DIM), BF16),
        compiler_params=_params("parallel", "parallel", "parallel"),
        name="mixer_a",
    )(slopes, sink.astype(F32), qkv, qkv, qkv, qkv, qkv, qkv, qkv)


def _mixer_b_kernel(q_ref, k_ref, v_ref, bias_ref, o_ref, *, rows, wr):
    scale = 1.0 / math.sqrt(HEAD_DIM)
    nkeys = wr * GRID_W

    def body(r, carry):
        rs = jnp.clip(r - wr // 2, 0, rows - wr)
        q = q_ref[pl.ds(pl.multiple_of(r * GRID_W, GRID_W), GRID_W), :]
        k = k_ref[pl.ds(pl.multiple_of(rs * GRID_W, GRID_W), nkeys), :]
        v = v_ref[pl.ds(pl.multiple_of(rs * GRID_W, GRID_W), nkeys), :]
        s = lax.dot_general(q, k, (((1,), (1,)), ((), ())), preferred_element_type=F32) * scale
        s = s + bias_ref[0, r - rs]
        mx = jnp.max(s, axis=-1, keepdims=True)
        p = jnp.exp(s - mx)
        p = p / jnp.sum(p, axis=-1, keepdims=True)
        o = jnp.dot(p.astype(BF16), v, preferred_element_type=F32)
        o_ref[pl.ds(pl.multiple_of(r * GRID_W, GRID_W), GRID_W), :] = o.astype(o_ref.dtype)
        return carry

    lax.fori_loop(0, rows, body, 0)


def _mixer_b_bias(rpb, wr):
    qc = np.arange(GRID_W)[:, None]
    kc = np.arange(GRID_W)[None, :]
    cstart = np.clip(qc - B_WIN_COLS // 2, 0, GRID_W - B_WIN_COLS)
    col_ok = (kc >= cstart) & (kc < cstart + B_WIN_COLS)
    dc_idx = np.clip(kc - qc + B_WIN_COLS - 1, 0, 2 * B_WIN_COLS - 2)
    variant = np.arange(wr)[:, None]
    w = np.arange(wr)[None, :]
    dr_idx = w - variant + B_WIN_ROWS - 1
    b = rpb.astype(F32)[:, dr_idx]
    b = b[:, :, :, dc_idx]
    b = jnp.where(jnp.asarray(col_ok)[None, None, None], b, NEG_INF)
    b = b.transpose(0, 1, 3, 2, 4)
    return b.reshape(rpb.shape[0], wr, GRID_W, wr * GRID_W)


def mixer_b(qkv, rpb, batch, seq):
    rows = seq // GRID_W
    wr = min(B_WIN_ROWS, rows)
    bias = _mixer_b_bias(rpb, wr)

    def col(off):
        return pl.BlockSpec((seq, HEAD_DIM), lambda b, h: (b, off + h))

    return pl.pallas_call(
        functools.partial(_mixer_b_kernel, rows=rows, wr=wr),
        grid=(batch, B_HEADS),
        in_specs=[col(0), col(B_HEADS), col(2 * B_HEADS),
                  pl.BlockSpec((1, wr, GRID_W, wr * GRID_W), lambda b, h: (h, 0, 0, 0))],
        out_specs=pl.BlockSpec((seq, HEAD_DIM), lambda b, h: (b, h)),
        out_shape=jax.ShapeDtypeStruct((batch * seq, B_HEADS * HEAD_DIM), BF16),
        compiler_params=_params("parallel", "parallel"),
        name="mixer_b",
    )(qkv, qkv, qkv, bias)


C_QW = 2 * HEAD_DIM


def _rope_fold(x, table):
    t = x * table
    return t + pltpu.roll(t, C_ROPE, axis=1)


def _mla_q_kernel(c_ref, g_ref, w_ref, t_ref, o_ref, *, scale):
    c = c_ref[...]
    inv = lax.rsqrt(jnp.mean(c * c, axis=-1, keepdims=True) + NORM_EPS)
    a = (c * inv * g_ref[...]).astype(BF16)
    table = t_ref[...]
    for h in range(C_HEADS):
        acc = jnp.dot(a, w_ref[:, h * C_QW:(h + 1) * C_QW], preferred_element_type=F32)
        o_ref[:, h * C_QW:h * C_QW + C_NOPE] = (acc[:, :C_NOPE] * scale).astype(o_ref.dtype)
        rope = _rope_fold(acc[:, C_NOPE:], table) * scale
        o_ref[:, h * C_QW + C_NOPE:(h + 1) * C_QW] = rope.astype(o_ref.dtype)


def mla_q_proj(c_all, q_norm, wq, table, seq, bm=512):
    m = c_all.shape[0]
    bm = _tile(min(m, seq), bm)
    nt = seq // bm
    scale = 1.0 / math.sqrt(C_NOPE + C_ROPE)
    return pl.pallas_call(
        functools.partial(_mla_q_kernel, scale=scale),
        grid=(m // bm,),
        in_specs=[pl.BlockSpec((bm, C_Q_RANK), lambda i: (i, 0)),
                  pl.BlockSpec((1, C_Q_RANK), lambda i: (0, 0)),
                  pl.BlockSpec((C_Q_RANK, C_HEADS * C_QW), lambda i: (0, 0)),
                  pl.BlockSpec((bm, LANES), lambda i: (i % nt, 0))],
        out_specs=pl.BlockSpec((bm, C_HEADS * C_QW), lambda i: (i, 0)),
        out_shape=jax.ShapeDtypeStruct((m, C_HEADS * C_QW), BF16),
        compiler_params=_params("parallel"),
        name="mla_q_proj",
    )(c_all, q_norm.reshape(1, C_Q_RANK).astype(F32), wq, table)


def _mla_kv_kernel(c_ref, g_ref, w_ref, kr_ref, t_ref, o_ref, okr_ref):
    c = c_ref[...]
    inv = lax.rsqrt(jnp.mean(c * c, axis=-1, keepdims=True) + NORM_EPS)
    a = (c * inv * g_ref[...]).astype(BF16)
    o_ref[...] = jnp.dot(a, w_ref[...], preferred_element_type=F32).astype(o_ref.dtype)
    rope = _rope_fold(kr_ref[...], t_ref[...])
    lane = lax.broadcasted_iota(jnp.int32, rope.shape, 1)
    okr_ref[...] = jnp.where(lane < C_ROPE, rope, 0.0).astype(okr_ref.dtype)


def mla_kv_proj(c_all, kv_norm, wkv, table, seq, bm=512):
    m = c_all.shape[0]
    bm = _tile(min(m, seq), bm)
    nt = seq // bm
    n = wkv.shape[1]
    ckv_blk = C_Q_RANK // C_KV_RANK
    kr_blk = (C_Q_RANK + C_KV_RANK) // LANES
    return pl.pallas_call(
        _mla_kv_kernel,
        grid=(m // bm,),
        in_specs=[pl.BlockSpec((bm, C_KV_RANK), lambda i: (i, ckv_blk)),
                  pl.BlockSpec((1, C_KV_RANK), lambda i: (0, 0)),
                  pl.BlockSpec((C_KV_RANK, n), lambda i: (0, 0)),
                  pl.BlockSpec((bm, LANES), lambda i: (i, kr_blk)),
                  pl.BlockSpec((bm, LANES), lambda i: (i % nt, 0))],
        out_specs=[pl.BlockSpec((bm, n), lambda i: (i, 0)),
                   pl.BlockSpec((bm, LANES), lambda i: (i, 0))],
        out_shape=[jax.ShapeDtypeStruct((m, n), BF16), jax.ShapeDtypeStruct((m, LANES), BF16)],
        compiler_params=_params("parallel"),
        name="mla_kv_proj",
    )(c_all, kv_norm.reshape(1, C_KV_RANK).astype(F32), wkv, c_all, table)


def _mla_attn_kernel(q_ref, kn_ref, kr_ref, v_ref, o_ref, m_ref, l_ref, acc_ref, *, bk, nkv):
    q = q_ref[...]
    m_ref[...] = jnp.full(m_ref.shape, NEG_INF, F32)
    l_ref[...] = jnp.zeros(l_ref.shape, F32)
    acc_ref[...] = jnp.zeros(acc_ref.shape, F32)

    def body(t, carry):
        off = pl.multiple_of(t * bk, bk)
        k = jnp.concatenate([kn_ref[pl.ds(off, bk), :], kr_ref[pl.ds(off, bk), :]], axis=1)
        s = lax.dot_general(q, k, (((1,), (1,)), ((), ())), preferred_element_type=F32)
        m_old = m_ref[...]
        m_new = jnp.maximum(m_old, jnp.max(s, axis=-1, keepdims=True))
        alpha = jnp.exp(m_old - m_new)
        p = jnp.exp(s - m_new)
        l_ref[...] = alpha * l_ref[...] + jnp.sum(p, axis=-1, keepdims=True)
        acc_ref[...] = alpha * acc_ref[...] + jnp.dot(p.astype(BF16), v_ref[pl.ds(off, bk), :],
                                                      preferred_element_type=F32)
        m_ref[...] = m_new
        return carry

    lax.fori_loop(0, nkv, body, 0)
    o_ref[...] = (acc_ref[...] / l_ref[...]).astype(o_ref.dtype)


def mla_attention(q, kv, kr, batch, seq, bq=1024, bk=512):
    bq = min(bq, seq)
    bk = min(bk, seq)
    nq = seq // bq
    return pl.pallas_call(
        functools.partial(_mla_attn_kernel, bk=bk, nkv=seq // bk),
        grid=(batch, C_HEADS, nq),
        in_specs=[pl.BlockSpec((bq, C_QW), lambda b, h, i: (b * nq + i, h)),
                  pl.BlockSpec((seq, C_NOPE), lambda b, h, i: (b, h)),
                  pl.BlockSpec((seq, LANES), lambda b, h, i: (b, 0)),
                  pl.BlockSpec((seq, C_V), lambda b, h, i: (b, C_HEADS + h))],
        out_specs=pl.BlockSpec((bq, C_V), lambda b, h, i: (b * nq + i, h)),
        out_shape=jax.ShapeDtypeStruct((batch * seq, C_HEADS * C_V), BF16),
        scratch_shapes=[pltpu.VMEM((bq, 1), F32), pltpu.VMEM((bq, 1), F32), pltpu.VMEM((bq, C_V), F32)],
        compiler_params=_params("parallel", "parallel", "parallel"),
        name="mla_attention",
    )(q, kv, kr, kv)


def _xattn_kernel(x_ref, g_ref, wq_ref, k_ref, v_ref, o_ref):
    x = x_ref[...]
    inv = lax.rsqrt(jnp.mean(x * x, axis=-1, keepdims=True) + NORM_EPS)
    h = (x * inv * g_ref[...]).astype(BF16)
    q = jnp.dot(h, wq_ref[...], preferred_element_type=F32).astype(BF16)
    scale = 1.0 / math.sqrt(HEAD_DIM)
    for hh in range(X_HEADS):
        sl = slice(hh * HEAD_DIM, (hh + 1) * HEAD_DIM)
        s = lax.dot_general(q[:, sl], k_ref[:, sl], (((1,), (1,)), ((), ())),
                            preferred_element_type=F32) * scale
        mx = jnp.max(s, axis=-1, keepdims=True)
        p = jnp.exp(s - mx)
        p = p / jnp.sum(p, axis=-1, keepdims=True)
        o = jnp.dot(p.astype(BF16), v_ref[:, sl], preferred_element_type=F32)
        o_ref[:, sl] = o.astype(o_ref.dtype)


def xattn_core(x, g, wq, kv_mem, seq, n_mem, bm=512):
    m, d = x.shape
    bm = _tile(min(m, seq), bm)
    per_b = seq // bm
    xd = X_HEADS * HEAD_DIM
    return pl.pallas_call(
        _xattn_kernel,
        grid=(m // bm,),
        in_specs=[pl.BlockSpec((bm, d), lambda i: (i, 0)),
                  pl.BlockSpec((1, d), lambda i: (0, 0)),
                  pl.BlockSpec((d, xd), lambda i: (0, 0)),
                  pl.BlockSpec((n_mem, xd), lambda i: (i // per_b, 0)),
                  pl.BlockSpec((n_mem, xd), lambda i: (i // per_b, 1))],
        out_specs=pl.BlockSpec((bm, xd), lambda i: (i, 0)),
        out_shape=jax.ShapeDtypeStruct((m, xd), BF16),
        compiler_params=_params("parallel"),
        name="xattn_core",
    )(x, g.reshape(1, d).astype(F32), wq, kv_mem, kv_mem)


def _rot_cols(w):
    half = w.shape[-1] // 2
    return jnp.concatenate([-w[..., half:], w[..., :half]], axis=-1)


def _prep_layer(w_in, c_w_q_b, c_w_kv_b, w_out, x_w_q, x_w_k, x_w_v, x_w_o, w_gate, w_up, w_down, ff_pad):
    d = w_in.shape[0]
    a_w = (A_Q_HEADS + 2 * A_KV_HEADS) * HEAD_DIM
    b_w = 3 * B_HEADS * HEAD_DIM
    c_off = a_w + b_w
    w_a = w_in[:, :a_w].astype(BF16)
    w_b = w_in[:, a_w:c_off].astype(BF16)
    w_ckr = w_in[:, c_off + C_Q_RANK + C_KV_RANK:]
    w_c = jnp.concatenate([w_in[:, c_off:], _rot_cols(w_ckr)], axis=1).astype(BF16)

    wq = c_w_q_b.reshape(C_Q_RANK, C_HEADS, C_NOPE + C_ROPE)
    wq_rope = wq[..., C_NOPE:]
    wq = jnp.concatenate([wq, _rot_cols(wq_rope)], axis=-1).reshape(C_Q_RANK, C_HEADS * C_QW).astype(BF16)

    wkv = c_w_kv_b.reshape(C_KV_RANK, C_HEADS, C_NOPE + C_V)
    wkv = jnp.concatenate([wkv[..., :C_NOPE].reshape(C_KV_RANK, C_HEADS * C_NOPE),
                           wkv[..., C_NOPE:].reshape(C_KV_RANK, C_HEADS * C_V)], axis=1).astype(BF16)

    ya_w = A_Q_HEADS * HEAD_DIM
    yb_w = B_HEADS * HEAD_DIM
    w_out = w_out.astype(BF16)
    w_o = (w_out[:ya_w], w_out[ya_w:ya_w + yb_w], w_out[ya_w + yb_w:])

    ff = w_gate.shape[1]
    pad = ff_pad - ff
    wg = jnp.pad(w_gate.astype(BF16), ((0, 0), (0, pad)))
    wu = jnp.pad(w_up.astype(BF16), ((0, 0), (0, pad)))
    wd = jnp.pad(w_down.astype(BF16), ((0, pad), (0, 0)))
    w_xkv = jnp.concatenate([x_w_k, x_w_v], axis=1).astype(BF16)
    return dict(w_a=w_a, w_b=w_b, w_c=w_c, wq=wq, wkv=wkv, w_o=w_o, wg=wg, wu=wu, wd=wd,
                x_w_q=x_w_q.astype(BF16), w_xkv=w_xkv, x_w_o=x_w_o.astype(BF16))


def _rope_table(seq):
    inv = 1.0 / (ROPE_BASE ** (jnp.arange(0, C_ROPE, 2, dtype=F32) / C_ROPE))
    ang = jnp.arange(seq, dtype=F32)[:, None] * inv[None, :]
    cos, sin = jnp.cos(ang), jnp.sin(ang)
    return jnp.concatenate([cos, cos, sin, sin], axis=1)


def kernel(x, mem, ln_mix, w_in, a_sink, b_rpb, c_q_norm, c_w_q_b, c_kv_norm, c_w_kv_b, w_out,
           ln_xattn, ln_mem, x_w_q, x_w_k, x_w_v, x_w_o, ln_ffn, w_gate, w_up, w_down, ln_final):
    batch, seq, d = x.shape
    n_mem = mem.shape[1]
    depth = w_in.shape[0]
    ff = w_gate.shape[2]
    ff_pad = -(-ff // 1024) * 1024
    table = _rope_table(seq)
    xs = x.reshape(batch * seq, d)
    mem2 = mem.reshape(batch * n_mem, d)
    for l in range(depth):
        w = _prep_layer(w_in[l], c_w_q_b[l], c_w_kv_b[l], w_out[l], x_w_q[l], x_w_k[l], x_w_v[l], x_w_o[l],
                        w_gate[l], w_up[l], w_down[l], ff_pad)
        h = rmsnorm(xs, ln_mix[l], BF16)
        qkv_a = matmul(h, w["w_a"], BF16, bn=1280, name="w_in_a")
        qkv_b = matmul(h, w["w_b"], BF16, name="w_in_b")
        c_all = matmul(h, w["w_c"], F32, bm=512, bn=w["w_c"].shape[1], name="w_in_c")
        ya = mixer_a(qkv_a, a_sink[l], batch, seq)
        yb = mixer_b(qkv_b, b_rpb[l], batch, seq)
        q_c = mla_q_proj(c_all, c_q_norm[l], w["wq"], table, seq)
        kv_c, kr_c = mla_kv_proj(c_all, c_kv_norm[l], w["wkv"], table, seq)
        yc = mla_attention(q_c, kv_c, kr_c, batch, seq)
        xs = matmul([ya, yb, yc], list(w["w_o"]), F32, res=xs, name="w_out")
        mem_n = rmsnorm(mem2, ln_mem[l], BF16)
        kv_mem = matmul(mem_n, w["w_xkv"], BF16, name="xattn_kv")
        o_x = xattn_core(xs, ln_xattn[l], w["x_w_q"], kv_mem, seq, n_mem)
        xs = matmul(o_x, w["x_w_o"], F32, res=xs, name="xattn_out")
        h = rmsnorm(xs, ln_ffn[l], BF16)
        act = gate_up(h, w["wg"], w["wu"])
        xs = matmul(act, w["wd"], F32, res=xs, bk=ff_pad // 4, name="ffn_down")
    out = rmsnorm(xs, ln_final, F32)
    return out.reshape(batch, seq, d)
```

```python
import functools
import math

import numpy as np
import jax
import jax.numpy as jnp
from jax import lax
from jax.experimental import pallas as pl
from jax.experimental.pallas import tpu as pltpu

F32 = jnp.float32
BF16 = jnp.bfloat16

HEAD_DIM = 128
GRID_W = 64
A_Q_HEADS = 12
A_KV_HEADS = 4
A_REP = A_Q_HEADS // A_KV_HEADS
A_WINDOW = 128
A_BLOCK = 128
B_HEADS = 8
B_WIN_ROWS = 8
B_WIN_COLS = 16
B_ROW_GROUP = 8
C_HEADS = 12
C_Q_RANK = 1024
C_KV_RANK = 512
C_NOPE = 128
C_ROPE = 64
C_V = 128
ROPE_BASE = 10000.0
X_HEADS = 4
NORM_EPS = 1e-6
NEG_INF = -1e30

VMEM_LIMIT_BYTES = 58 * 1024 * 1024
LANES = 128


def _params(*sem):
    return pltpu.CompilerParams(dimension_semantics=sem, vmem_limit_bytes=VMEM_LIMIT_BYTES)


def _tile(n, pref):
    if n <= pref:
        return n
    t = (pref // LANES) * LANES
    while t >= LANES:
        if n % t == 0:
            return t
        t -= LANES
    return n


def _rmsnorm_kernel(x_ref, g_ref, o_ref):
    x = x_ref[...].astype(F32)
    inv = lax.rsqrt(jnp.mean(x * x, axis=-1, keepdims=True) + NORM_EPS)
    o_ref[...] = (x * inv * g_ref[...]).astype(o_ref.dtype)


def rmsnorm(x, g, out_dtype, bm=256):
    m, d = x.shape
    bm = min(bm, m)
    return pl.pallas_call(
        _rmsnorm_kernel,
        grid=(m // bm,),
        in_specs=[pl.BlockSpec((bm, d), lambda i: (i, 0)), pl.BlockSpec((1, d), lambda i: (0, 0))],
        out_specs=pl.BlockSpec((bm, d), lambda i: (i, 0)),
        out_shape=jax.ShapeDtypeStruct((m, d), out_dtype),
        compiler_params=_params("parallel"),
        name="rmsnorm",
    )(x, g.reshape(1, d).astype(F32))


def _mm_kernel(*refs, n_a, nk, has_res):
    a_refs = refs[:n_a]
    w_refs = refs[n_a:2 * n_a]
    pos = 2 * n_a
    r_ref = refs[pos] if has_res else None
    pos += int(has_res)
    o_ref = refs[pos]
    acc_ref = refs[pos + 1] if nk > 1 else None

    part = jnp.dot(a_refs[0][...], w_refs[0][...], preferred_element_type=F32)
    for a_ref, w_ref in zip(a_refs[1:], w_refs[1:]):
        part = part + jnp.dot(a_ref[...], w_ref[...], preferred_element_type=F32)

    def finish(acc):
        if has_res:
            acc = r_ref[...] + acc
        o_ref[...] = acc.astype(o_ref.dtype)

    if nk == 1:
        finish(part)
    else:
        k = pl.program_id(2)

        @pl.when(k == 0)
        def _():
            acc_ref[...] = part

        @pl.when(jnp.logical_and(k > 0, k < nk - 1))
        def _():
            acc_ref[...] += part

        @pl.when(k == nk - 1)
        def _():
            finish(acc_ref[...] + part)


def matmul(a_list, w_list, out_dtype, res=None, bm=1024, bn=1024, bk=None, name="matmul"):
    if not isinstance(a_list, (list, tuple)):
        a_list, w_list = [a_list], [w_list]
    m = a_list[0].shape[0]
    n = w_list[0].shape[1]
    bm = _tile(m, bm)
    bn = _tile(n, bn)
    n_a = len(a_list)
    if bk is None or n_a > 1:
        nk = 1
    else:
        kdim = a_list[0].shape[1]
        assert kdim % bk == 0
        nk = kdim // bk
    in_specs = []
    for a in a_list:
        kd = a.shape[1] if nk == 1 else bk
        in_specs.append(pl.BlockSpec((bm, kd), lambda i, j, k: (i, k)))
    for w in w_list:
        kd = w.shape[0] if nk == 1 else bk
        in_specs.append(pl.BlockSpec((kd, bn), lambda i, j, k: (k, j)))
    args = list(a_list) + list(w_list)
    if res is not None:
        in_specs.append(pl.BlockSpec((bm, bn), lambda i, j, k: (i, j)))
        args.append(res)
    scratch = [pltpu.VMEM((bm, bn), F32)] if nk > 1 else []
    return pl.pallas_call(
        functools.partial(_mm_kernel, n_a=n_a, nk=nk, has_res=res is not None),
        grid=(m // bm, n // bn, nk),
        in_specs=in_specs,
        out_specs=pl.BlockSpec((bm, bn), lambda i, j, k: (i, j)),
        out_shape=jax.ShapeDtypeStruct((m, n), out_dtype),
        scratch_shapes=scratch,
        compiler_params=_params("parallel", "parallel", "arbitrary"),
        name=name,
    )(*args)


def _gate_up_kernel(h_ref, wg_ref, wu_ref, o_ref):
    h = h_ref[...]
    g = jnp.dot(h, wg_ref[...], preferred_element_type=F32)
    u = jnp.dot(h, wu_ref[...], preferred_element_type=F32)
    o_ref[...] = (jax.nn.silu(g) * u).astype(o_ref.dtype)


def gate_up(h, wg, wu, bm=1024, bn=512):
    m, d = h.shape
    n = wg.shape[1]
    bm = _tile(m, bm)
    bn = _tile(n, bn)
    return pl.pallas_call(
        _gate_up_kernel,
        grid=(m // bm, n // bn),
        in_specs=[
            pl.BlockSpec((bm, d), lambda i, j: (i, 0)),
            pl.BlockSpec((d, bn), lambda i, j: (0, j)),
            pl.BlockSpec((d, bn), lambda i, j: (0, j)),
        ],
        out_specs=pl.BlockSpec((bm, bn), lambda i, j: (i, j)),
        out_shape=jax.ShapeDtypeStruct((m, n), BF16),
        compiler_params=_params("parallel", "parallel"),
        name="gate_up",
    )(h, wg, wu)


def _mixer_a_kernel(slope_ref, sink_ref, q_ref, kp_ref, kc_ref, kn_ref, vp_ref, vc_ref, vn_ref, o_ref,
                    *, tq, seq):
    i = pl.program_id(1)
    g = pl.program_id(2)
    nsub = tq // A_BLOCK
    kcat = jnp.concatenate([kp_ref[...], kc_ref[...], kn_ref[...]], axis=0)
    vcat = jnp.concatenate([vp_ref[...], vc_ref[...], vn_ref[...]], axis=0)
    qi = lax.broadcasted_iota(jnp.int32, (A_BLOCK, 3 * A_BLOCK), 0)
    kj = lax.broadcasted_iota(jnp.int32, (A_BLOCK, 3 * A_BLOCK), 1)
    rel = kj - A_BLOCK - qi
    dist = jnp.abs(rel)
    dist_f = dist.astype(F32)
    in_window = dist <= A_WINDOW
    scale = 1.0 / math.sqrt(HEAD_DIM)
    units = [(j, r) for j in range(nsub) for r in range(A_REP)]
    scores = []
    for j, r in units:
        kband = kcat[j * A_BLOCK:(j + 3) * A_BLOCK]
        q = q_ref[j * A_BLOCK:(j + 1) * A_BLOCK, r * HEAD_DIM:(r + 1) * HEAD_DIM]
        scores.append(lax.dot_general(q, kband, (((1,), (1,)), ((), ())), preferred_element_type=F32))
    probs = []
    for (j, r), s in zip(units, scores):
        head = g * A_REP + r
        sink = sink_ref[head]
        kpos = (i * nsub + j - 1) * A_BLOCK + kj
        valid = in_window & (kpos >= 0) & (kpos < seq)
        s = s * scale - slope_ref[head] * dist_f
        s = jnp.where(valid, s, NEG_INF)
        mx = jnp.maximum(jnp.max(s, axis=-1, keepdims=True), sink)
        p = jnp.exp(s - mx)
        denom = jnp.sum(p, axis=-1, keepdims=True) + jnp.exp(sink - mx)
        probs.append((p / denom).astype(BF16))
    for (j, r), p in zip(units, probs):
        vband = vcat[j * A_BLOCK:(j + 3) * A_BLOCK]
        o = jnp.dot(p, vband, preferred_element_type=F32)
        o_ref[j * A_BLOCK:(j + 1) * A_BLOCK, r * HEAD_DIM:(r + 1) * HEAD_DIM] = o.astype(o_ref.dtype)


def mixer_a(qkv, sink, batch, seq, tq=512):
    tq = min(tq, seq)
    nq = seq // tq
    nsub = tq // A_BLOCK
    nblk = seq // A_BLOCK
    kcol = A_Q_HEADS
    vcol = A_Q_HEADS + A_KV_HEADS
    slopes = 2.0 ** (-8.0 * jnp.arange(1, A_Q_HEADS + 1, dtype=F32) / A_Q_HEADS)

    def cur(col):
        return pl.BlockSpec((tq, HEAD_DIM), lambda b, i, g: (b * nq + i, col + g))

    def prev(col):
        return pl.BlockSpec((A_BLOCK, HEAD_DIM),
                            lambda b, i, g: (b * nblk + jnp.maximum(i * nsub - 1, 0), col + g))

    def nxt(col):
        return pl.BlockSpec((A_BLOCK, HEAD_DIM),
                            lambda b, i, g: (b * nblk + jnp.minimum((i + 1) * nsub, nblk - 1), col + g))

    smem = pl.BlockSpec(memory_space=pltpu.SMEM)
    return pl.pallas_call(
        functools.partial(_mixer_a_kernel, tq=tq, seq=seq),
        grid=(batch, nq, A_KV_HEADS),
        in_specs=[smem, smem,
                  pl.BlockSpec((tq, A_REP * HEAD_DIM), lambda b, i, g: (b * nq + i, g)),
                  prev(kcol), cur(kcol), nxt(kcol), prev(vcol), cur(vcol), nxt(vcol)],
        out_specs=pl.BlockSpec((tq, A_REP * HEAD_DIM), lambda b, i, g: (b * nq + i, g)),
        out_shape=jax.ShapeDtypeStruct((batch * seq, A_Q_HEADS * HEAD_DIM), BF16),
        compiler_params=_params("parallel", "parallel", "parallel"),
        name="mixer_a",
    )(slopes, sink.astype(F32), qkv, qkv, qkv, qkv, qkv, qkv, qkv)


def _mixer_b_kernel(q_ref, k_ref, v_ref, bias_ref, o_ref, *, rows, wr, group):
    scale = 1.0 / math.sqrt(HEAD_DIM)
    nkeys = wr * GRID_W

    def row_slice(r, n):
        return pl.ds(pl.multiple_of(r * GRID_W, GRID_W), n)

    def body(gi, carry):
        rws = [gi * group + j for j in range(group)]
        starts = [jnp.clip(r - wr // 2, 0, rows - wr) for r in rws]
        scores = [lax.dot_general(q_ref[row_slice(r, GRID_W), :], k_ref[row_slice(rs, nkeys), :],
                                  (((1,), (1,)), ((), ())), preferred_element_type=F32)
                  for r, rs in zip(rws, starts)]
        probs = []
        for r, rs, s in zip(rws, starts, scores):
            s = s * scale + bias_ref[0, r - rs]
            mx = jnp.max(s, axis=-1, keepdims=True)
            p = jnp.exp(s - mx)
            probs.append((p / jnp.sum(p, axis=-1, keepdims=True)).astype(BF16))
        for r, rs, p in zip(rws, starts, probs):
            o = jnp.dot(p, v_ref[row_slice(rs, nkeys), :], preferred_element_type=F32)
            o_ref[row_slice(r, GRID_W), :] = o.astype(o_ref.dtype)
        return carry

    lax.fori_loop(0, rows // group, body, 0)


def _mixer_b_bias(rpb, wr):
    qc = np.arange(GRID_W)[:, None]
    kc = np.arange(GRID_W)[None, :]
    cstart = np.clip(qc - B_WIN_COLS // 2, 0, GRID_W - B_WIN_COLS)
    col_ok = (kc >= cstart) & (kc < cstart + B_WIN_COLS)
    dc_idx = np.clip(kc - qc + B_WIN_COLS - 1, 0, 2 * B_WIN_COLS - 2)
    variant = np.arange(wr)[:, None]
    w = np.arange(wr)[None, :]
    dr_idx = w - variant + B_WIN_ROWS - 1
    b = rpb.astype(F32)[:, dr_idx]
    b = b[:, :, :, dc_idx]
    b = jnp.where(jnp.asarray(col_ok)[None, None, None], b, NEG_INF)
    b = b.transpose(0, 1, 3, 2, 4)
    return b.reshape(rpb.shape[0], wr, GRID_W, wr * GRID_W)


def mixer_b(qkv, rpb, batch, seq):
    rows = seq // GRID_W
    wr = min(B_WIN_ROWS, rows)
    bias = _mixer_b_bias(rpb, wr)
    group = math.gcd(rows, B_ROW_GROUP)

    def col(off):
        return pl.BlockSpec((seq, HEAD_DIM), lambda b, h: (b, off + h))

    return pl.pallas_call(
        functools.partial(_mixer_b_kernel, rows=rows, wr=wr, group=group),
        grid=(batch, B_HEADS),
        in_specs=[col(0), col(B_HEADS), col(2 * B_HEADS),
                  pl.BlockSpec((1, wr, GRID_W, wr * GRID_W), lambda b, h: (h, 0, 0, 0))],
        out_specs=pl.BlockSpec((seq, HEAD_DIM), lambda b, h: (b, h)),
        out_shape=jax.ShapeDtypeStruct((batch * seq, B_HEADS * HEAD_DIM), BF16),
        compiler_params=_params("parallel", "parallel"),
        name="mixer_b",
    )(qkv, qkv, qkv, bias)


C_QW = 2 * HEAD_DIM
C_ONES_ROWS = 16
C_KV_TILE = 1024
LOG2E = 1.4426950408889634
_NT = (((1,), (1,)), ((), ()))


def _rms_bf16(c, g):
    inv = lax.rsqrt(jnp.mean(c * c, axis=-1, keepdims=True) + NORM_EPS)
    return (c * inv * g).astype(BF16)


def _mla_q_kernel(c_ref, g_ref, w_ref, t_ref, o_ref, *, scale):
    a = _rms_bf16(c_ref[...], g_ref[...])
    table = t_ref[...]
    for h in range(C_HEADS):
        r0 = h * C_QW
        acc = lax.dot_general(w_ref[r0:r0 + C_QW, :], a, _NT, preferred_element_type=F32)
        o_ref[r0:r0 + C_NOPE, :] = (acc[:C_NOPE] * scale).astype(o_ref.dtype)
        t = acc[C_NOPE:] * table
        rope = (t[:C_ROPE] + t[C_ROPE:]) * scale
        o_ref[r0 + C_NOPE:r0 + C_NOPE + C_ROPE, :] = rope.astype(o_ref.dtype)
        o_ref[r0 + C_NOPE + C_ROPE:r0 + C_QW, :] = jnp.zeros((C_ROPE, rope.shape[1]), o_ref.dtype)


def mla_q_proj(c_all, q_norm, wq_t, table_t, seq, bm=512):
    m = c_all.shape[0]
    bm = _tile(min(m, seq), bm)
    nt = seq // bm
    scale = LOG2E / math.sqrt(C_NOPE + C_ROPE)
    return pl.pallas_call(
        functools.partial(_mla_q_kernel, scale=scale),
        grid=(m // bm,),
        in_specs=[pl.BlockSpec((bm, C_Q_RANK), lambda i: (i, 0)),
                  pl.BlockSpec((1, C_Q_RANK), lambda i: (0, 0)),
                  pl.BlockSpec((C_HEADS * C_QW, C_Q_RANK), lambda i: (0, 0)),
                  pl.BlockSpec((LANES, bm), lambda i: (0, i % nt))],
        out_specs=pl.BlockSpec((C_HEADS * C_QW, bm), lambda i: (0, i)),
        out_shape=jax.ShapeDtypeStruct((C_HEADS * C_QW, m), BF16),
        compiler_params=_params("parallel"),
        name="mla_q_proj",
    )(c_all, q_norm.reshape(1, C_Q_RANK).astype(F32), wq_t, table_t)


def _mla_kv_kernel(c_ref, g_ref, wk_ref, wvt_ref, kr_ref, t_ref, ok_ref, ovt_ref, okr_ref):
    a = _rms_bf16(c_ref[...], g_ref[...])
    ok_ref[...] = jnp.dot(a, wk_ref[...], preferred_element_type=F32).astype(ok_ref.dtype)
    ovt_ref[0] = lax.dot_general(wvt_ref[...], a, _NT, preferred_element_type=F32).astype(ovt_ref.dtype)
    t = kr_ref[...] * t_ref[...]
    rope = t + pltpu.roll(t, C_ROPE, axis=1)
    lane = lax.broadcasted_iota(jnp.int32, rope.shape, 1)
    okr_ref[...] = jnp.where(lane < C_ROPE, rope, 0.0).astype(okr_ref.dtype)


def mla_kv_proj(c_all, kv_norm, wk, wv_t, table, seq, bm):
    m = c_all.shape[0]
    nt = seq // bm
    n = wk.shape[1]
    ckv_blk = C_Q_RANK // C_KV_RANK
    kr_blk = (C_Q_RANK + C_KV_RANK) // LANES
    return pl.pallas_call(
        _mla_kv_kernel,
        grid=(m // bm,),
        in_specs=[pl.BlockSpec((bm, C_KV_RANK), lambda i: (i, ckv_blk)),
                  pl.BlockSpec((1, C_KV_RANK), lambda i: (0, 0)),
                  pl.BlockSpec((C_KV_RANK, n), lambda i: (0, 0)),
                  pl.BlockSpec((n, C_KV_RANK), lambda i: (0, 0)),
                  pl.BlockSpec((bm, LANES), lambda i: (i, kr_blk)),
                  pl.BlockSpec((bm, LANES), lambda i: (i % nt, 0))],
        out_specs=[pl.BlockSpec((bm, n), lambda i: (i, 0)),
                   pl.BlockSpec((1, n, bm), lambda i: (i, 0, 0)),
                   pl.BlockSpec((bm, LANES), lambda i: (i, 0))],
        out_shape=[jax.ShapeDtypeStruct((m, n), BF16),
                   jax.ShapeDtypeStruct((m // bm, n, bm), BF16),
                   jax.ShapeDtypeStruct((m, LANES), BF16)],
        compiler_params=_params("parallel"),
        name="mla_kv_proj",
    )(c_all, kv_norm.reshape(1, C_KV_RANK).astype(F32), wk, wv_t, c_all, table)


def _mla_attn_kernel(qt_ref, kn_ref, kr_ref, vt_ref, o_ref, s_ref, m_ref, acc_ref, *, bk, nkv, cw):
    bq = qt_ref.shape[1]
    cols = [slice(c * cw, (c + 1) * cw) for c in range(bq // cw)]
    m_ref[...] = jnp.full(m_ref.shape, NEG_INF, F32)
    acc_ref[...] = jnp.zeros(acc_ref.shape, F32)
    ones = jnp.ones((C_ONES_ROWS, bk), BF16)

    def keys(t):
        off = pl.multiple_of(t * bk, bk)
        return jnp.concatenate([kn_ref[pl.ds(off, bk), :], kr_ref[pl.ds(off, bk), :]], axis=1)

    def softmax_update(cs):
        s = s_ref[:, cs]
        m_old = m_ref[:, cs]
        m_new = jnp.maximum(m_old, jnp.max(s, axis=0, keepdims=True))
        m_ref[:, cs] = m_new
        return jnp.exp2(s - m_new).astype(BF16), jnp.exp2(m_old - m_new)

    def step(t, with_next):
        vt = jnp.concatenate([vt_ref[t], ones], axis=0)
        k_next = keys(t + 1) if with_next else None
        p, alpha = softmax_update(cols[0])
        for c, cs in enumerate(cols):
            if with_next:
                s_ref[:, cs] = jnp.dot(k_next, qt_ref[:, cs], preferred_element_type=F32)
            if c + 1 < len(cols):
                p_next, alpha_next = softmax_update(cols[c + 1])
            acc_ref[:, cs] = alpha * acc_ref[:, cs] + jnp.dot(vt, p, preferred_element_type=F32)
            if c + 1 < len(cols):
                p, alpha = p_next, alpha_next

    k0 = keys(0)
    for cs in cols:
        s_ref[:, cs] = jnp.dot(k0, qt_ref[:, cs], preferred_element_type=F32)

    def body(t, carry):
        step(t, True)
        return carry

    lax.fori_loop(0, nkv - 1, body, 0)
    step(nkv - 1, False)
    out_t = acc_ref[:C_V, :] / acc_ref[C_V:C_V + 1, :]
    o_ref[...] = out_t.T.astype(o_ref.dtype)


def mla_attention(q_t, kn, kr, v_t, batch, seq, bk, bq=1024, cw=256):
    bq = min(bq, seq)
    cw = min(cw, bq)
    nq = seq // bq
    nkv = seq // bk
    return pl.pallas_call(
        functools.partial(_mla_attn_kernel, bk=bk, nkv=nkv, cw=cw),
        grid=(batch, C_HEADS, nq),
        in_specs=[pl.BlockSpec((C_QW, bq), lambda b, h, i: (h, b * nq + i)),
                  pl.BlockSpec((seq, C_NOPE), lambda b, h, i: (b, h)),
                  pl.BlockSpec((seq, LANES), lambda b, h, i: (b, 0)),
                  pl.BlockSpec((nkv, C_V, bk), lambda b, h, i: (b, h, 0))],
        out_specs=pl.BlockSpec((bq, C_V), lambda b, h, i: (b * nq + i, h)),
        out_shape=jax.ShapeDtypeStruct((batch * seq, C_HEADS * C_V), BF16),
        scratch_shapes=[pltpu.VMEM((bk, bq), F32), pltpu.VMEM((1, bq), F32),
                        pltpu.VMEM((C_V + C_ONES_ROWS, bq), F32)],
        compiler_params=_params("parallel", "parallel", "arbitrary"),
        name="mla_attention",
    )(q_t, kn, kr, v_t)


def _xattn_kernel(x_ref, g_ref, wq_ref, k_ref, v_ref, o_ref):
    x = x_ref[...]
    inv = lax.rsqrt(jnp.mean(x * x, axis=-1, keepdims=True) + NORM_EPS)
    h = (x * inv * g_ref[...]).astype(BF16)
    q = jnp.dot(h, wq_ref[...], preferred_element_type=F32).astype(BF16)
    scale = 1.0 / math.sqrt(HEAD_DIM)
    heads = [slice(hh * HEAD_DIM, (hh + 1) * HEAD_DIM) for hh in range(X_HEADS)]
    scores = [lax.dot_general(q[:, sl], k_ref[:, sl], (((1,), (1,)), ((), ())), preferred_element_type=F32)
              for sl in heads]
    probs = []
    for s in scores:
        s = s * scale
        mx = jnp.max(s, axis=-1, keepdims=True)
        p = jnp.exp(s - mx)
        probs.append((p / jnp.sum(p, axis=-1, keepdims=True)).astype(BF16))
    for sl, p in zip(heads, probs):
        o_ref[:, sl] = jnp.dot(p, v_ref[:, sl], preferred_element_type=F32).astype(o_ref.dtype)


def xattn_core(x, g, wq, kv_mem, seq, n_mem, bm=512):
    m, d = x.shape
    bm = _tile(min(m, seq), bm)
    per_b = seq // bm
    xd = X_HEADS * HEAD_DIM
    return pl.pallas_call(
        _xattn_kernel,
        grid=(m // bm,),
        in_specs=[pl.BlockSpec((bm, d), lambda i: (i, 0)),
                  pl.BlockSpec((1, d), lambda i: (0, 0)),
                  pl.BlockSpec((d, xd), lambda i: (0, 0)),
                  pl.BlockSpec((n_mem, xd), lambda i: (i // per_b, 0)),
                  pl.BlockSpec((n_mem, xd), lambda i: (i // per_b, 1))],
        out_specs=pl.BlockSpec((bm, xd), lambda i: (i, 0)),
        out_shape=jax.ShapeDtypeStruct((m, xd), BF16),
        compiler_params=_params("parallel"),
        name="xattn_core",
    )(x, g.reshape(1, d).astype(F32), wq, kv_mem, kv_mem)


def _rot_cols(w):
    half = w.shape[-1] // 2
    return jnp.concatenate([-w[..., half:], w[..., :half]], axis=-1)


def _prep_layer(w_in, c_w_q_b, c_w_kv_b, w_out, x_w_q, x_w_k, x_w_v, x_w_o, w_gate, w_up, w_down, ff_pad):
    d = w_in.shape[0]
    a_w = (A_Q_HEADS + 2 * A_KV_HEADS) * HEAD_DIM
    b_w = 3 * B_HEADS * HEAD_DIM
    c_off = a_w + b_w
    w_a = w_in[:, :a_w].astype(BF16)
    w_b = w_in[:, a_w:c_off].astype(BF16)
    w_ckr = w_in[:, c_off + C_Q_RANK + C_KV_RANK:]
    w_c = jnp.concatenate([w_in[:, c_off:], _rot_cols(w_ckr)], axis=1).astype(BF16)

    wq = c_w_q_b.reshape(C_Q_RANK, C_HEADS, C_NOPE + C_ROPE)
    wq_rope = wq[..., C_NOPE:]
    wq_t = jnp.concatenate([wq, _rot_cols(wq_rope)], axis=-1).reshape(C_Q_RANK, C_HEADS * C_QW).astype(BF16).T

    wkv = c_w_kv_b.reshape(C_KV_RANK, C_HEADS, C_NOPE + C_V).astype(BF16)
    wk = wkv[..., :C_NOPE].reshape(C_KV_RANK, C_HEADS * C_NOPE)
    wv_t = wkv[..., C_NOPE:].reshape(C_KV_RANK, C_HEADS * C_V).T

    ya_w = A_Q_HEADS * HEAD_DIM
    yb_w = B_HEADS * HEAD_DIM
    w_out = w_out.astype(BF16)
    w_o = (w_out[:ya_w], w_out[ya_w:ya_w + yb_w], w_out[ya_w + yb_w:])

    ff = w_gate.shape[1]
    pad = ff_pad - ff
    wg = jnp.pad(w_gate.astype(BF16), ((0, 0), (0, pad)))
    wu = jnp.pad(w_up.astype(BF16), ((0, 0), (0, pad)))
    wd = jnp.pad(w_down.astype(BF16), ((0, pad), (0, 0)))
    w_xkv = jnp.concatenate([x_w_k, x_w_v], axis=1).astype(BF16)
    return dict(w_a=w_a, w_b=w_b, w_c=w_c, wq_t=wq_t, wk=wk, wv_t=wv_t, w_o=w_o, wg=wg, wu=wu, wd=wd,
                x_w_q=x_w_q.astype(BF16), w_xkv=w_xkv, x_w_o=x_w_o.astype(BF16))


def _rope_table(seq):
    inv = 1.0 / (ROPE_BASE ** (jnp.arange(0, C_ROPE, 2, dtype=F32) / C_ROPE))
    ang = jnp.arange(seq, dtype=F32)[:, None] * inv[None, :]
    cos, sin = jnp.cos(ang), jnp.sin(ang)
    return jnp.concatenate([cos, cos, sin, sin], axis=1)


def kernel(x, mem, ln_mix, w_in, a_sink, b_rpb, c_q_norm, c_w_q_b, c_kv_norm, c_w_kv_b, w_out,
           ln_xattn, ln_mem, x_w_q, x_w_k, x_w_v, x_w_o, ln_ffn, w_gate, w_up, w_down, ln_final):
    batch, seq, d = x.shape
    n_mem = mem.shape[1]
    depth = w_in.shape[0]
    ff = w_gate.shape[2]
    ff_pad = -(-ff // 1024) * 1024
    table = _rope_table(seq)
    table_t = table.T
    kv_tile = min(C_KV_TILE, seq)
    xs = x.reshape(batch * seq, d)
    mem2 = mem.reshape(batch * n_mem, d)
    for l in range(depth):
        w = _prep_layer(w_in[l], c_w_q_b[l], c_w_kv_b[l], w_out[l], x_w_q[l], x_w_k[l], x_w_v[l], x_w_o[l],
                        w_gate[l], w_up[l], w_down[l], ff_pad)
        h = rmsnorm(xs, ln_mix[l], BF16)
        qkv_a = matmul(h, w["w_a"], BF16, bn=1280, name="w_in_a")
        qkv_b = matmul(h, w["w_b"], BF16, name="w_in_b")
        c_all = matmul(h, w["w_c"], F32, bm=512, bn=w["w_c"].shape[1], name="w_in_c")
        ya = mixer_a(qkv_a, a_sink[l], batch, seq)
        yb = mixer_b(qkv_b, b_rpb[l], batch, seq)
        q_t = mla_q_proj(c_all, c_q_norm[l], w["wq_t"], table_t, seq)
        kn_c, v_t, kr_c = mla_kv_proj(c_all, c_kv_norm[l], w["wk"], w["wv_t"], table, seq, kv_tile)
        yc = mla_attention(q_t, kn_c, kr_c, v_t, batch, seq, kv_tile)
        xs = matmul([ya, yb, yc], list(w["w_o"]), F32, res=xs, name="w_out")
        mem_n = rmsnorm(mem2, ln_mem[l], BF16)
        kv_mem = matmul(mem_n, w["w_xkv"], BF16, name="xattn_kv")
        o_x = xattn_core(xs, ln_xattn[l], w["x_w_q"], kv_mem, seq, n_mem)
        xs = matmul(o_x, w["x_w_o"], F32, res=xs, name="xattn_out")
        h = rmsnorm(xs, ln_ffn[l], BF16)
        act = gate_up(h, w["wg"], w["wu"])
        xs = matmul(act, w["wd"], F32, res=xs, bk=ff_pad // 4, name="ffn_down")
    out = rmsnorm(xs, ln_final, F32)
    return out.reshape(batch, seq, d)
```

```python
import functools
import math

import numpy as np
import jax
import jax.numpy as jnp
from jax import lax
from jax.experimental import pallas as pl
from jax.experimental.pallas import tpu as pltpu

F32 = jnp.float32
BF16 = jnp.bfloat16

HEAD_DIM = 128
GRID_W = 64
A_Q_HEADS = 12
A_KV_HEADS = 4
A_REP = A_Q_HEADS // A_KV_HEADS
A_WINDOW = 128
A_BLOCK = 128
B_HEADS = 8
B_WIN_ROWS = 8
B_WIN_COLS = 16
B_ROW_GROUP = 8
C_HEADS = 12
C_Q_RANK = 1024
C_KV_RANK = 512
C_NOPE = 128
C_ROPE = 64
C_V = 128
ROPE_BASE = 10000.0
X_HEADS = 4
NORM_EPS = 1e-6
NEG_INF = -1e30

VMEM_LIMIT_BYTES = 58 * 1024 * 1024
LANES = 128


def _params(*sem):
    return pltpu.CompilerParams(dimension_semantics=sem, vmem_limit_bytes=VMEM_LIMIT_BYTES)


def _tile(n, pref):
    if n <= pref:
        return n
    t = (pref // LANES) * LANES
    while t >= LANES:
        if n % t == 0:
            return t
        t -= LANES
    return n


def _rmsnorm_kernel(x_ref, g_ref, o_ref):
    x = x_ref[...].astype(F32)
    inv = lax.rsqrt(jnp.mean(x * x, axis=-1, keepdims=True) + NORM_EPS)
    o_ref[...] = (x * inv * g_ref[...]).astype(o_ref.dtype)


def rmsnorm(x, g, out_dtype, bm=256):
    m, d = x.shape
    bm = min(bm, m)
    return pl.pallas_call(
        _rmsnorm_kernel,
        grid=(m // bm,),
        in_specs=[pl.BlockSpec((bm, d), lambda i: (i, 0)), pl.BlockSpec((1, d), lambda i: (0, 0))],
        out_specs=pl.BlockSpec((bm, d), lambda i: (i, 0)),
        out_shape=jax.ShapeDtypeStruct((m, d), out_dtype),
        compiler_params=_params("parallel"),
        name="rmsnorm",
    )(x, g.reshape(1, d).astype(F32))


def _cast_pad_cols_kernel(x_ref, o_ref):
    n = x_ref.shape[-1]
    o_ref[:, :n] = x_ref[...].astype(o_ref.dtype)
    if o_ref.shape[1] > n:
        o_ref[:, n:] = jnp.zeros((o_ref.shape[0], o_ref.shape[1] - n), o_ref.dtype)


def cast_pad_cols(w, n_pad, br=256):
    nl, r, n = w.shape
    br = _tile(r, br)
    return pl.pallas_call(
        _cast_pad_cols_kernel,
        grid=(nl, r // br),
        in_specs=[pl.BlockSpec((None, br, n), lambda l, i: (l, i, 0))],
        out_specs=pl.BlockSpec((None, br, n_pad), lambda l, i: (l, i, 0)),
        out_shape=jax.ShapeDtypeStruct((nl, r, n_pad), BF16),
        compiler_params=_params("parallel", "parallel"),
        name="cast_pad_cols",
    )(w)


def _cast_pad_rows_kernel(x_ref, o_ref, *, n_src_blocks):
    i = pl.program_id(1)

    @pl.when(i < n_src_blocks)
    def _():
        o_ref[...] = x_ref[...].astype(o_ref.dtype)

    @pl.when(i >= n_src_blocks)
    def _():
        o_ref[...] = jnp.zeros(o_ref.shape, o_ref.dtype)


def cast_pad_rows(w, r_pad, br=256):
    nl, r, n = w.shape
    br = math.gcd(math.gcd(r, r_pad), br)
    nsrc = r // br
    return pl.pallas_call(
        functools.partial(_cast_pad_rows_kernel, n_src_blocks=nsrc),
        grid=(nl, r_pad // br),
        in_specs=[pl.BlockSpec((None, br, n), lambda l, i: (l, jnp.minimum(i, nsrc - 1), 0))],
        out_specs=pl.BlockSpec((None, br, n), lambda l, i: (l, i, 0)),
        out_shape=jax.ShapeDtypeStruct((nl, r_pad, n), BF16),
        compiler_params=_params("parallel", "parallel"),
        name="cast_pad_rows",
    )(w)


def _mm_kernel(*refs, n_a, nk, has_res):
    a_refs = refs[:n_a]
    w_ref = refs[n_a]
    pos = n_a + 1
    r_ref = refs[pos] if has_res else None
    pos += int(has_res)
    o_ref = refs[pos]
    acc_ref = refs[pos + 1] if nk > 1 else None

    a = a_refs[0][...] if n_a == 1 else jnp.concatenate([r[...] for r in a_refs], axis=1)
    part = jnp.dot(a, w_ref[...], preferred_element_type=F32)

    def finish(acc):
        if has_res:
            acc = r_ref[...] + acc
        o_ref[...] = acc.astype(o_ref.dtype)

    if nk == 1:
        finish(part)
    else:
        k = pl.program_id(2)

        @pl.when(k == 0)
        def _():
            acc_ref[...] = part

        @pl.when(jnp.logical_and(k > 0, k < nk - 1))
        def _():
            acc_ref[...] += part

        @pl.when(k == nk - 1)
        def _():
            finish(acc_ref[...] + part)


def matmul(a_list, w, layer, out_dtype, *, n=None, col0=0, res=None, bm=1024, bn=1024, bk=None, name="matmul"):
    if not isinstance(a_list, (list, tuple)):
        a_list = [a_list]
    m = a_list[0].shape[0]
    kdim = sum(a.shape[1] for a in a_list)
    assert w.shape[1] == kdim
    n = w.shape[2] - col0 if n is None else n
    bm = _tile(m, bm)
    bn = _tile(math.gcd(n, col0) if col0 else n, bn)
    assert n % bn == 0 and col0 % bn == 0
    n_a = len(a_list)
    if bk is None or n_a > 1:
        nk = 1
    else:
        assert kdim % bk == 0
        nk = kdim // bk
    in_specs = []
    for a in a_list:
        kd = a.shape[1] if nk == 1 else bk
        in_specs.append(pl.BlockSpec((bm, kd), lambda i, j, k: (i, k)))
    cb0 = col0 // bn
    in_specs.append(pl.BlockSpec((None, kdim if nk == 1 else bk, bn), lambda i, j, k: (layer, k, cb0 + j)))
    args = list(a_list) + [w]
    if res is not None:
        in_specs.append(pl.BlockSpec((bm, bn), lambda i, j, k: (i, j)))
        args.append(res)
    scratch = [pltpu.VMEM((bm, bn), F32)] if nk > 1 else []
    return pl.pallas_call(
        functools.partial(_mm_kernel, n_a=n_a, nk=nk, has_res=res is not None),
        grid=(m // bm, n // bn, nk),
        in_specs=in_specs,
        out_specs=pl.BlockSpec((bm, bn), lambda i, j, k: (i, j)),
        out_shape=jax.ShapeDtypeStruct((m, n), out_dtype),
        scratch_shapes=scratch,
        compiler_params=_params("parallel", "parallel", "arbitrary"),
        name=name,
    )(*args)


def _gate_up_kernel(h_ref, wg_ref, wu_ref, o_ref):
    h = h_ref[...]
    g = jnp.dot(h, wg_ref[...], preferred_element_type=F32)
    u = jnp.dot(h, wu_ref[...], preferred_element_type=F32)
    o_ref[...] = (jax.nn.silu(g) * u).astype(o_ref.dtype)


def gate_up(h, wg, wu, layer, bm=1024, bn=512):
    m, d = h.shape
    n = wg.shape[2]
    bm = _tile(m, bm)
    bn = _tile(n, bn)
    return pl.pallas_call(
        _gate_up_kernel,
        grid=(m // bm, n // bn),
        in_specs=[
            pl.BlockSpec((bm, d), lambda i, j: (i, 0)),
            pl.BlockSpec((None, d, bn), lambda i, j: (layer, 0, j)),
            pl.BlockSpec((None, d, bn), lambda i, j: (layer, 0, j)),
        ],
        out_specs=pl.BlockSpec((bm, bn), lambda i, j: (i, j)),
        out_shape=jax.ShapeDtypeStruct((m, n), BF16),
        compiler_params=_params("parallel", "parallel"),
        name="gate_up",
    )(h, wg, wu)


def _mixer_a_kernel(slope_ref, sink_ref, q_ref, kp_ref, kc_ref, kn_ref, vp_ref, vc_ref, vn_ref, o_ref,
                    *, tq, seq):
    i = pl.program_id(1)
    g = pl.program_id(2)
    nsub = tq // A_BLOCK
    kcat = jnp.concatenate([kp_ref[...], kc_ref[...], kn_ref[...]], axis=0)
    vcat = jnp.concatenate([vp_ref[...], vc_ref[...], vn_ref[...]], axis=0)
    qi = lax.broadcasted_iota(jnp.int32, (A_BLOCK, 3 * A_BLOCK), 0)
    kj = lax.broadcasted_iota(jnp.int32, (A_BLOCK, 3 * A_BLOCK), 1)
    rel = kj - A_BLOCK - qi
    dist = jnp.abs(rel)
    dist_f = dist.astype(F32)
    in_window = dist <= A_WINDOW
    scale = 1.0 / math.sqrt(HEAD_DIM)
    units = [(j, r) for j in range(nsub) for r in range(A_REP)]
    scores = []
    for j, r in units:
        kband = kcat[j * A_BLOCK:(j + 3) * A_BLOCK]
        q = q_ref[j * A_BLOCK:(j + 1) * A_BLOCK, r * HEAD_DIM:(r + 1) * HEAD_DIM]
        scores.append(lax.dot_general(q, kband, (((1,), (1,)), ((), ())), preferred_element_type=F32))
    probs = []
    for (j, r), s in zip(units, scores):
        head = g * A_REP + r
        sink = sink_ref[head]
        kpos = (i * nsub + j - 1) * A_BLOCK + kj
        valid = in_window & (kpos >= 0) & (kpos < seq)
        s = s * scale - slope_ref[head] * dist_f
        s = jnp.where(valid, s, NEG_INF)
        mx = jnp.maximum(jnp.max(s, axis=-1, keepdims=True), sink)
        p = jnp.exp(s - mx)
        denom = jnp.sum(p, axis=-1, keepdims=True) + jnp.exp(sink - mx)
        probs.append((p / denom).astype(BF16))
    for (j, r), p in zip(units, probs):
        vband = vcat[j * A_BLOCK:(j + 3) * A_BLOCK]
        o = jnp.dot(p, vband, preferred_element_type=F32)
        o_ref[j * A_BLOCK:(j + 1) * A_BLOCK, r * HEAD_DIM:(r + 1) * HEAD_DIM] = o.astype(o_ref.dtype)


def mixer_a(qkv, sink, batch, seq, tq=512):
    tq = min(tq, seq)
    nq = seq // tq
    nsub = tq // A_BLOCK
    nblk = seq // A_BLOCK
    kcol = A_Q_HEADS
    vcol = A_Q_HEADS + A_KV_HEADS
    slopes = 2.0 ** (-8.0 * jnp.arange(1, A_Q_HEADS + 1, dtype=F32) / A_Q_HEADS)

    def cur(col):
        return pl.BlockSpec((tq, HEAD_DIM), lambda b, i, g: (b * nq + i, col + g))

    def prev(col):
        return pl.BlockSpec((A_BLOCK, HEAD_DIM),
                            lambda b, i, g: (b * nblk + jnp.maximum(i * nsub - 1, 0), col + g))

    def nxt(col):
        return pl.BlockSpec((A_BLOCK, HEAD_DIM),
                            lambda b, i, g: (b * nblk + jnp.minimum((i + 1) * nsub, nblk - 1), col + g))

    smem = pl.BlockSpec(memory_space=pltpu.SMEM)
    return pl.pallas_call(
        functools.partial(_mixer_a_kernel, tq=tq, seq=seq),
        grid=(batch, nq, A_KV_HEADS),
        in_specs=[smem, smem,
                  pl.BlockSpec((tq, A_REP * HEAD_DIM), lambda b, i, g: (b * nq + i, g)),
                  prev(kcol), cur(kcol), nxt(kcol), prev(vcol), cur(vcol), nxt(vcol)],
        out_specs=pl.BlockSpec((tq, A_REP * HEAD_DIM), lambda b, i, g: (b * nq + i, g)),
        out_shape=jax.ShapeDtypeStruct((batch * seq, A_Q_HEADS * HEAD_DIM), BF16),
        compiler_params=_params("parallel", "parallel", "parallel"),
        name="mixer_a",
    )(slopes, sink.astype(F32), qkv, qkv, qkv, qkv, qkv, qkv, qkv)


def _mixer_b_kernel(q_ref, k_ref, v_ref, bias_ref, o_ref, *, rows, wr, group):
    scale = 1.0 / math.sqrt(HEAD_DIM)
    nkeys = wr * GRID_W

    def row_slice(r, n):
        return pl.ds(pl.multiple_of(r * GRID_W, GRID_W), n)

    def body(gi, carry):
        rws = [gi * group + j for j in range(group)]
        starts = [jnp.clip(r - wr // 2, 0, rows - wr) for r in rws]
        scores = [lax.dot_general(q_ref[row_slice(r, GRID_W), :], k_ref[row_slice(rs, nkeys), :],
                                  (((1,), (1,)), ((), ())), preferred_element_type=F32)
                  for r, rs in zip(rws, starts)]
        probs = []
        for r, rs, s in zip(rws, starts, scores):
            s = s * scale + bias_ref[0, r - rs]
            mx = jnp.max(s, axis=-1, keepdims=True)
            p = jnp.exp(s - mx)
            probs.append((p / jnp.sum(p, axis=-1, keepdims=True)).astype(BF16))
        for r, rs, p in zip(rws, starts, probs):
            o = jnp.dot(p, v_ref[row_slice(rs, nkeys), :], preferred_element_type=F32)
            o_ref[row_slice(r, GRID_W), :] = o.astype(o_ref.dtype)
        return carry

    lax.fori_loop(0, rows // group, body, 0)


def _mixer_b_bias(rpb, wr):
    qc = np.arange(GRID_W)[:, None]
    kc = np.arange(GRID_W)[None, :]
    cstart = np.clip(qc - B_WIN_COLS // 2, 0, GRID_W - B_WIN_COLS)
    col_ok = (kc >= cstart) & (kc < cstart + B_WIN_COLS)
    lo = GRID_W - B_WIN_COLS
    tbl = jnp.pad(rpb.astype(F32), ((0, 0), (0, 0), (0, 0), (lo, lo)), mode="edge")
    b = jnp.stack([tbl[..., GRID_W - 1 - q:2 * GRID_W - 1 - q] for q in range(GRID_W)], axis=-2)
    b = jnp.stack([b[:, :, B_WIN_ROWS - 1 - v:B_WIN_ROWS - 1 - v + wr] for v in range(wr)], axis=2)
    b = jnp.where(jnp.asarray(col_ok), b, NEG_INF)
    b = b.transpose(0, 1, 2, 4, 3, 5)
    return b.reshape(rpb.shape[0], rpb.shape[1], wr, GRID_W, wr * GRID_W)


def mixer_b(qkv, bias, batch, seq):
    rows = seq // GRID_W
    wr = min(B_WIN_ROWS, rows)
    group = math.gcd(rows, B_ROW_GROUP)

    def col(off):
        return pl.BlockSpec((seq, HEAD_DIM), lambda b, h: (b, off + h))

    return pl.pallas_call(
        functools.partial(_mixer_b_kernel, rows=rows, wr=wr, group=group),
        grid=(batch, B_HEADS),
        in_specs=[col(0), col(B_HEADS), col(2 * B_HEADS),
                  pl.BlockSpec((1, wr, GRID_W, wr * GRID_W), lambda b, h: (h, 0, 0, 0))],
        out_specs=pl.BlockSpec((seq, HEAD_DIM), lambda b, h: (b, h)),
        out_shape=jax.ShapeDtypeStruct((batch * seq, B_HEADS * HEAD_DIM), BF16),
        compiler_params=_params("parallel", "parallel"),
        name="mixer_b",
    )(qkv, qkv, qkv, bias)


C_QW = 2 * HEAD_DIM
C_ONES_ROWS = 16
C_KV_TILE = 2048
LOG2E = 1.4426950408889634
_NT = (((1,), (1,)), ((), ()))


def _rms_bf16(c, g):
    inv = lax.rsqrt(jnp.mean(c * c, axis=-1, keepdims=True) + NORM_EPS)
    return (c * inv * g).astype(BF16)


def _mla_q_kernel(c_ref, g_ref, w_ref, t_ref, o_ref, *, scale):
    a = _rms_bf16(c_ref[...], g_ref[...])
    table = t_ref[...]
    for h in range(C_HEADS):
        r0 = h * C_QW
        acc = lax.dot_general(w_ref[r0:r0 + C_QW, :], a, _NT, preferred_element_type=F32)
        o_ref[r0:r0 + C_NOPE, :] = (acc[:C_NOPE] * scale).astype(o_ref.dtype)
        t = acc[C_NOPE:] * table
        rope = (t[:C_ROPE] + t[C_ROPE:]) * scale
        o_ref[r0 + C_NOPE:r0 + C_NOPE + C_ROPE, :] = rope.astype(o_ref.dtype)
        o_ref[r0 + C_NOPE + C_ROPE:r0 + C_QW, :] = jnp.zeros((C_ROPE, rope.shape[1]), o_ref.dtype)


def mla_q_proj(c_all, q_norm, wq_t, layer, table_t, seq, bm=512):
    m = c_all.shape[0]
    bm = _tile(min(m, seq), bm)
    nt = seq // bm
    scale = LOG2E / math.sqrt(C_NOPE + C_ROPE)
    return pl.pallas_call(
        functools.partial(_mla_q_kernel, scale=scale),
        grid=(m // bm,),
        in_specs=[pl.BlockSpec((bm, C_Q_RANK), lambda i: (i, 0)),
                  pl.BlockSpec((1, C_Q_RANK), lambda i: (0, 0)),
                  pl.BlockSpec((None, C_HEADS * C_QW, C_Q_RANK), lambda i: (layer, 0, 0)),
                  pl.BlockSpec((LANES, bm), lambda i: (0, i % nt))],
        out_specs=pl.BlockSpec((C_HEADS * C_QW, bm), lambda i: (0, i)),
        out_shape=jax.ShapeDtypeStruct((C_HEADS * C_QW, m), BF16),
        compiler_params=_params("parallel"),
        name="mla_q_proj",
    )(c_all, q_norm.reshape(1, C_Q_RANK).astype(F32), wq_t, table_t)


def _mla_kv_kernel(c_ref, g_ref, wk_ref, wvt_ref, kr_ref, t_ref, ok_ref, ovt_ref, okr_ref):
    a = _rms_bf16(c_ref[...], g_ref[...])
    ok_ref[...] = jnp.dot(a, wk_ref[...], preferred_element_type=F32).astype(ok_ref.dtype)
    ovt_ref[0] = lax.dot_general(wvt_ref[...], a, _NT, preferred_element_type=F32).astype(ovt_ref.dtype)
    t = kr_ref[...] * t_ref[...]
    rope = t + pltpu.roll(t, C_ROPE, axis=1)
    lane = lax.broadcasted_iota(jnp.int32, rope.shape, 1)
    okr_ref[...] = jnp.where(lane < C_ROPE, rope, 0.0).astype(okr_ref.dtype)


def mla_kv_proj(c_all, kv_norm, wk, wv_t, layer, table, seq, bm):
    m = c_all.shape[0]
    nt = seq // bm
    n = wk.shape[2]
    ckv_blk = C_Q_RANK // C_KV_RANK
    kr_blk = (C_Q_RANK + C_KV_RANK) // LANES
    return pl.pallas_call(
        _mla_kv_kernel,
        grid=(m // bm,),
        in_specs=[pl.BlockSpec((bm, C_KV_RANK), lambda i: (i, ckv_blk)),
                  pl.BlockSpec((1, C_KV_RANK), lambda i: (0, 0)),
                  pl.BlockSpec((None, C_KV_RANK, n), lambda i: (layer, 0, 0)),
                  pl.BlockSpec((None, n, C_KV_RANK), lambda i: (layer, 0, 0)),
                  pl.BlockSpec((bm, LANES), lambda i: (i, kr_blk)),
                  pl.BlockSpec((bm, LANES), lambda i: (i % nt, 0))],
        out_specs=[pl.BlockSpec((bm, n), lambda i: (i, 0)),
                   pl.BlockSpec((1, n, bm), lambda i: (i, 0, 0)),
                   pl.BlockSpec((bm, LANES), lambda i: (i, 0))],
        out_shape=[jax.ShapeDtypeStruct((m, n), BF16),
                   jax.ShapeDtypeStruct((m // bm, n, bm), BF16),
                   jax.ShapeDtypeStruct((m, LANES), BF16)],
        compiler_params=_params("parallel"),
        name="mla_kv_proj",
    )(c_all, kv_norm.reshape(1, C_KV_RANK).astype(F32), wk, wv_t, c_all, table)


def _mla_attn_kernel(qt_ref, kn_ref, kr_ref, vt_ref, o_ref, s_ref, m_ref, acc_ref, *, bk, nkv, cw):
    bq = qt_ref.shape[1]
    cols = [slice(c * cw, (c + 1) * cw) for c in range(bq // cw)]
    m_ref[...] = jnp.full(m_ref.shape, NEG_INF, F32)
    acc_ref[...] = jnp.zeros(acc_ref.shape, F32)
    ones = jnp.ones((C_ONES_ROWS, bk), BF16)

    def keys(t):
        off = pl.multiple_of(t * bk, bk)
        return jnp.concatenate([kn_ref[pl.ds(off, bk), :], kr_ref[pl.ds(off, bk), :]], axis=1)

    def softmax_update(cs):
        s = s_ref[:, cs]
        m_old = m_ref[:, cs]
        m_new = jnp.maximum(m_old, jnp.max(s, axis=0, keepdims=True))
        m_ref[:, cs] = m_new
        return jnp.exp2(s - m_new).astype(BF16), jnp.exp2(m_old - m_new)

    def step(t, with_next):
        vt = jnp.concatenate([vt_ref[t], ones], axis=0)
        k_next = keys(t + 1) if with_next else None
        p, alpha = softmax_update(cols[0])
        for c, cs in enumerate(cols):
            if with_next:
                s_ref[:, cs] = jnp.dot(k_next, qt_ref[:, cs], preferred_element_type=F32)
            if c + 1 < len(cols):
                p_next, alpha_next = softmax_update(cols[c + 1])
            acc_ref[:, cs] = alpha * acc_ref[:, cs] + jnp.dot(vt, p, preferred_element_type=F32)
            if c + 1 < len(cols):
                p, alpha = p_next, alpha_next

    k0 = keys(0)
    for cs in cols:
        s_ref[:, cs] = jnp.dot(k0, qt_ref[:, cs], preferred_element_type=F32)

    def body(t, carry):
        step(t, True)
        return carry

    lax.fori_loop(0, nkv - 1, body, 0)
    step(nkv - 1, False)
    out_t = acc_ref[:C_V, :] / acc_ref[C_V:C_V + 1, :]
    o_ref[...] = out_t.T.astype(o_ref.dtype)


def mla_attention(q_t, kn, kr, v_t, batch, seq, bk, bq=1024, cw=256):
    bq = min(bq, seq)
    cw = min(cw, bq)
    nq = seq // bq
    nkv = seq // bk
    return pl.pallas_call(
        functools.partial(_mla_attn_kernel, bk=bk, nkv=nkv, cw=cw),
        grid=(batch, C_HEADS, nq),
        in_specs=[pl.BlockSpec((C_QW, bq), lambda b, h, i: (h, b * nq + i)),
                  pl.BlockSpec((seq, C_NOPE), lambda b, h, i: (b, h)),
                  pl.BlockSpec((seq, LANES), lambda b, h, i: (b, 0)),
                  pl.BlockSpec((nkv, C_V, bk), lambda b, h, i: (b, h, 0))],
        out_specs=pl.BlockSpec((bq, C_V), lambda b, h, i: (b * nq + i, h)),
        out_shape=jax.ShapeDtypeStruct((batch * seq, C_HEADS * C_V), BF16),
        scratch_shapes=[pltpu.VMEM((bk, bq), F32), pltpu.VMEM((1, bq), F32),
                        pltpu.VMEM((C_V + C_ONES_ROWS, bq), F32)],
        compiler_params=_params("parallel", "parallel", "arbitrary"),
        name="mla_attention",
    )(q_t, kn, kr, v_t)


def _xattn_kernel(x_ref, g_ref, wq_ref, k_ref, v_ref, wo_ref, g2_ref, xo_ref, ho_ref, o_scr):
    x = x_ref[...]
    h = _rms_bf16(x, g_ref[...])
    q = jnp.dot(h, wq_ref[...], preferred_element_type=F32).astype(BF16)
    scale = 1.0 / math.sqrt(HEAD_DIM)
    heads = [slice(hh * HEAD_DIM, (hh + 1) * HEAD_DIM) for hh in range(X_HEADS)]
    scores = [lax.dot_general(q[:, sl], k_ref[:, sl], (((1,), (1,)), ((), ())), preferred_element_type=F32)
              for sl in heads]
    probs = []
    for s in scores:
        s = s * scale
        mx = jnp.max(s, axis=-1, keepdims=True)
        p = jnp.exp(s - mx)
        probs.append((p / jnp.sum(p, axis=-1, keepdims=True)).astype(BF16))
    for sl, p in zip(heads, probs):
        o_scr[:, sl] = jnp.dot(p, v_ref[:, sl], preferred_element_type=F32).astype(o_scr.dtype)
    x_new = x + jnp.dot(o_scr[...], wo_ref[...], preferred_element_type=F32)
    xo_ref[...] = x_new
    ho_ref[...] = _rms_bf16(x_new, g2_ref[...])


def xattn_block(x, g, wq, kv_mem, wo, g_next, layer, seq, n_mem, bm=512):
    m, d = x.shape
    bm = _tile(min(m, seq), bm)
    per_b = seq // bm
    xd = X_HEADS * HEAD_DIM
    once = pl.Buffered(1)
    return pl.pallas_call(
        _xattn_kernel,
        grid=(m // bm,),
        in_specs=[pl.BlockSpec((bm, d), lambda i: (i, 0)),
                  pl.BlockSpec((1, d), lambda i: (0, 0)),
                  pl.BlockSpec((None, d, xd), lambda i: (layer, 0, 0), pipeline_mode=once),
                  pl.BlockSpec((n_mem, xd), lambda i: (i // per_b, 0)),
                  pl.BlockSpec((n_mem, xd), lambda i: (i // per_b, 1)),
                  pl.BlockSpec((None, xd, d), lambda i: (layer, 0, 0), pipeline_mode=once),
                  pl.BlockSpec((1, d), lambda i: (0, 0))],
        out_specs=[pl.BlockSpec((bm, d), lambda i: (i, 0)), pl.BlockSpec((bm, d), lambda i: (i, 0))],
        out_shape=[jax.ShapeDtypeStruct((m, d), F32), jax.ShapeDtypeStruct((m, d), BF16)],
        scratch_shapes=[pltpu.VMEM((bm, xd), BF16)],
        compiler_params=_params("parallel"),
        name="xattn_block",
    )(x, g.reshape(1, d).astype(F32), wq, kv_mem, kv_mem, wo, g_next.reshape(1, d).astype(F32))


def _rot_cols(w):
    half = w.shape[-1] // 2
    return jnp.concatenate([-w[..., half:], w[..., :half]], axis=-1)


A_IN_W = (A_Q_HEADS + 2 * A_KV_HEADS) * HEAD_DIM
B_IN_W = 3 * B_HEADS * HEAD_DIM


def _prep_weights(w_in, c_w_q_b, c_w_kv_b, w_out, x_w_q, x_w_k, x_w_v, x_w_o, w_gate, w_up, w_down, ff_pad):
    nl = w_in.shape[0]
    c_off = A_IN_W + B_IN_W
    w_ckr = w_in[:, :, c_off + C_Q_RANK + C_KV_RANK:]
    w_c = jnp.concatenate([w_in[:, :, c_off:], _rot_cols(w_ckr)], axis=2).astype(BF16)

    wq = c_w_q_b.reshape(nl, C_Q_RANK, C_HEADS, C_NOPE + C_ROPE)
    wq = jnp.concatenate([wq, _rot_cols(wq[..., C_NOPE:])], axis=-1)
    wq_t = wq.reshape(nl, C_Q_RANK, C_HEADS * C_QW).astype(BF16).transpose(0, 2, 1)

    wkv = c_w_kv_b.reshape(nl, C_KV_RANK, C_HEADS, C_NOPE + C_V).astype(BF16)
    wk = wkv[..., :C_NOPE].reshape(nl, C_KV_RANK, C_HEADS * C_NOPE)
    wv_t = wkv[..., C_NOPE:].reshape(nl, C_KV_RANK, C_HEADS * C_V).transpose(0, 2, 1)

    return dict(w_in=w_in.astype(BF16), w_c=w_c, wq_t=wq_t, wk=wk, wv_t=wv_t, w_out=w_out.astype(BF16),
                wg=cast_pad_cols(w_gate, ff_pad), wu=cast_pad_cols(w_up, ff_pad), wd=cast_pad_rows(w_down, ff_pad),
                x_w_q=x_w_q.astype(BF16), w_xkv=jnp.concatenate([x_w_k, x_w_v], axis=2).astype(BF16),
                x_w_o=x_w_o.astype(BF16))


def _rope_table(seq):
    inv = 1.0 / (ROPE_BASE ** (jnp.arange(0, C_ROPE, 2, dtype=F32) / C_ROPE))
    ang = jnp.arange(seq, dtype=F32)[:, None] * inv[None, :]
    cos, sin = jnp.cos(ang), jnp.sin(ang)
    return jnp.concatenate([cos, cos, sin, sin], axis=1)


def kernel(x, mem, ln_mix, w_in, a_sink, b_rpb, c_q_norm, c_w_q_b, c_kv_norm, c_w_kv_b, w_out,
           ln_xattn, ln_mem, x_w_q, x_w_k, x_w_v, x_w_o, ln_ffn, w_gate, w_up, w_down, ln_final):
    batch, seq, d = x.shape
    n_mem = mem.shape[1]
    depth = w_in.shape[0]
    ff = w_gate.shape[2]
    ff_pad = -(-ff // 1024) * 1024
    table = _rope_table(seq)
    table_t = table.T
    kv_tile = min(C_KV_TILE, seq)
    xs = x.reshape(batch * seq, d)
    mem2 = mem.reshape(batch * n_mem, d)
    w = _prep_weights(w_in, c_w_q_b, c_w_kv_b, w_out, x_w_q, x_w_k, x_w_v, x_w_o, w_gate, w_up, w_down, ff_pad)
    b_bias = _mixer_b_bias(b_rpb, min(B_WIN_ROWS, seq // GRID_W))
    for l in range(depth):
        h = rmsnorm(xs, ln_mix[l], BF16)
        qkv_a = matmul(h, w["w_in"], l, BF16, n=A_IN_W, bn=1280, name="w_in_a")
        qkv_b = matmul(h, w["w_in"], l, BF16, n=B_IN_W, col0=A_IN_W, name="w_in_b")
        c_all = matmul(h, w["w_c"], l, F32, bm=512, bn=w["w_c"].shape[2], name="w_in_c")
        ya = mixer_a(qkv_a, a_sink[l], batch, seq)
        yb = mixer_b(qkv_b, b_bias[l], batch, seq)
        q_t = mla_q_proj(c_all, c_q_norm[l], w["wq_t"], l, table_t, seq)
        kn_c, v_t, kr_c = mla_kv_proj(c_all, c_kv_norm[l], w["wk"], w["wv_t"], l, table, seq, kv_tile)
        yc = mla_attention(q_t, kn_c, kr_c, v_t, batch, seq, kv_tile)
        xs = matmul([ya, yb, yc], w["w_out"], l, F32, res=xs, name="w_out")
        mem_n = rmsnorm(mem2, ln_mem[l], BF16)
        kv_mem = matmul(mem_n, w["w_xkv"], l, BF16, name="xattn_kv")
        xs, h = xattn_block(xs, ln_xattn[l], w["x_w_q"], kv_mem, w["x_w_o"], ln_ffn[l], l, seq, n_mem)
        act = gate_up(h, w["wg"], w["wu"], l)
        xs = matmul(act, w["wd"], l, F32, res=xs, bk=ff_pad // 4, name="ffn_down")
    out = rmsnorm(xs, ln_final, F32)
    return out.reshape(batch, seq, d)
```

```python
import functools
import math

import numpy as np
import jax
import jax.numpy as jnp
from jax import lax
from jax.experimental import pallas as pl
from jax.experimental.pallas import tpu as pltpu

F32 = jnp.float32
BF16 = jnp.bfloat16

HEAD_DIM = 128
GRID_W = 64
A_Q_HEADS = 12
A_KV_HEADS = 4
A_REP = A_Q_HEADS // A_KV_HEADS
A_WINDOW = 128
A_BLOCK = 128
B_HEADS = 8
B_WIN_ROWS = 8
B_WIN_COLS = 16
B_ROW_GROUP = 8
C_HEADS = 12
C_Q_RANK = 1024
C_KV_RANK = 512
C_NOPE = 128
C_ROPE = 64
C_V = 128
ROPE_BASE = 10000.0
X_HEADS = 4
NORM_EPS = 1e-6
NEG_INF = -1e30
LOG2E = 1.4426950408889634

VMEM_LIMIT_BYTES = 58 * 1024 * 1024
LANES = 128


def _params(*sem):
    return pltpu.CompilerParams(dimension_semantics=sem, vmem_limit_bytes=VMEM_LIMIT_BYTES)


def _tile(n, pref):
    if n <= pref:
        return n
    t = (pref // LANES) * LANES
    while t >= LANES:
        if n % t == 0:
            return t
        t -= LANES
    return n


def _rmsnorm_kernel(x_ref, g_ref, o_ref):
    x = x_ref[...].astype(F32)
    inv = lax.rsqrt(jnp.mean(x * x, axis=-1, keepdims=True) + NORM_EPS)
    o_ref[...] = (x * inv * g_ref[...]).astype(o_ref.dtype)


def rmsnorm(x, g, out_dtype, bm=256):
    m, d = x.shape
    bm = min(bm, m)
    return pl.pallas_call(
        _rmsnorm_kernel,
        grid=(m // bm,),
        in_specs=[pl.BlockSpec((bm, d), lambda i: (i, 0)), pl.BlockSpec((1, d), lambda i: (0, 0))],
        out_specs=pl.BlockSpec((bm, d), lambda i: (i, 0)),
        out_shape=jax.ShapeDtypeStruct((m, d), out_dtype),
        compiler_params=_params("parallel"),
        name="rmsnorm",
    )(x, g.reshape(1, d).astype(F32))


def _cast_pad_cols_kernel(x_ref, o_ref):
    n = x_ref.shape[-1]
    o_ref[:, :n] = x_ref[...].astype(o_ref.dtype)
    if o_ref.shape[1] > n:
        o_ref[:, n:] = jnp.zeros((o_ref.shape[0], o_ref.shape[1] - n), o_ref.dtype)


def cast_pad_cols(w, n_pad, br=256):
    nl, r, n = w.shape
    br = _tile(r, br)
    return pl.pallas_call(
        _cast_pad_cols_kernel,
        grid=(nl, r // br),
        in_specs=[pl.BlockSpec((None, br, n), lambda l, i: (l, i, 0))],
        out_specs=pl.BlockSpec((None, br, n_pad), lambda l, i: (l, i, 0)),
        out_shape=jax.ShapeDtypeStruct((nl, r, n_pad), BF16),
        compiler_params=_params("parallel", "parallel"),
        name="cast_pad_cols",
    )(w)


def _cast_pad_rows_kernel(x_ref, o_ref, *, n_src_blocks):
    i = pl.program_id(1)

    @pl.when(i < n_src_blocks)
    def _():
        o_ref[...] = x_ref[...].astype(o_ref.dtype)

    @pl.when(i >= n_src_blocks)
    def _():
        o_ref[...] = jnp.zeros(o_ref.shape, o_ref.dtype)


def cast_pad_rows(w, r_pad, br=256):
    nl, r, n = w.shape
    br = math.gcd(math.gcd(r, r_pad), br)
    nsrc = r // br
    return pl.pallas_call(
        functools.partial(_cast_pad_rows_kernel, n_src_blocks=nsrc),
        grid=(nl, r_pad // br),
        in_specs=[pl.BlockSpec((None, br, n), lambda l, i: (l, jnp.minimum(i, nsrc - 1), 0))],
        out_specs=pl.BlockSpec((None, br, n), lambda l, i: (l, i, 0)),
        out_shape=jax.ShapeDtypeStruct((nl, r_pad, n), BF16),
        compiler_params=_params("parallel", "parallel"),
        name="cast_pad_rows",
    )(w)


def _mm_kernel(*refs, n_a, nk, has_res):
    a_refs = refs[:n_a]
    w_ref = refs[n_a]
    pos = n_a + 1
    r_ref = refs[pos] if has_res else None
    pos += int(has_res)
    o_ref = refs[pos]
    acc_ref = refs[pos + 1] if nk > 1 else None

    a = a_refs[0][...] if n_a == 1 else jnp.concatenate([r[...] for r in a_refs], axis=1)
    part = jnp.dot(a, w_ref[...], preferred_element_type=F32)

    def finish(acc):
        if has_res:
            acc = r_ref[...] + acc
        o_ref[...] = acc.astype(o_ref.dtype)

    if nk == 1:
        finish(part)
    else:
        k = pl.program_id(2)

        @pl.when(k == 0)
        def _():
            acc_ref[...] = part

        @pl.when(jnp.logical_and(k > 0, k < nk - 1))
        def _():
            acc_ref[...] += part

        @pl.when(k == nk - 1)
        def _():
            finish(acc_ref[...] + part)


def matmul(a_list, w, layer, out_dtype, *, n=None, col0=0, res=None, bm=1024, bn=1024, bk=None, name="matmul"):
    if not isinstance(a_list, (list, tuple)):
        a_list = [a_list]
    m = a_list[0].shape[0]
    kdim = sum(a.shape[1] for a in a_list)
    assert w.shape[1] == kdim
    n = w.shape[2] - col0 if n is None else n
    bm = _tile(m, bm)
    bn = _tile(math.gcd(n, col0) if col0 else n, bn)
    assert n % bn == 0 and col0 % bn == 0
    n_a = len(a_list)
    if bk is None or n_a > 1:
        nk = 1
    else:
        assert kdim % bk == 0
        nk = kdim // bk
    in_specs = []
    for a in a_list:
        kd = a.shape[1] if nk == 1 else bk
        in_specs.append(pl.BlockSpec((bm, kd), lambda i, j, k: (i, k)))
    cb0 = col0 // bn
    in_specs.append(pl.BlockSpec((None, kdim if nk == 1 else bk, bn), lambda i, j, k: (layer, k, cb0 + j)))
    args = list(a_list) + [w]
    if res is not None:
        in_specs.append(pl.BlockSpec((bm, bn), lambda i, j, k: (i, j)))
        args.append(res)
    scratch = [pltpu.VMEM((bm, bn), F32)] if nk > 1 else []
    return pl.pallas_call(
        functools.partial(_mm_kernel, n_a=n_a, nk=nk, has_res=res is not None),
        grid=(m // bm, n // bn, nk),
        in_specs=in_specs,
        out_specs=pl.BlockSpec((bm, bn), lambda i, j, k: (i, j)),
        out_shape=jax.ShapeDtypeStruct((m, n), out_dtype),
        scratch_shapes=scratch,
        compiler_params=_params("parallel", "parallel", "arbitrary"),
        name=name,
    )(*args)


def _gate_up_kernel(h_ref, wg_ref, wu_ref, o_ref):
    h = h_ref[...]
    g = jnp.dot(h, wg_ref[...], preferred_element_type=F32)
    u = jnp.dot(h, wu_ref[...], preferred_element_type=F32)
    o_ref[...] = (jax.nn.silu(g) * u).astype(o_ref.dtype)


def gate_up(h, wg, wu, layer, bm=1024, bn=512):
    m, d = h.shape
    n = wg.shape[2]
    bm = _tile(m, bm)
    bn = _tile(n, bn)
    return pl.pallas_call(
        _gate_up_kernel,
        grid=(m // bm, n // bn),
        in_specs=[
            pl.BlockSpec((bm, d), lambda i, j: (i, 0)),
            pl.BlockSpec((None, d, bn), lambda i, j: (layer, 0, j)),
            pl.BlockSpec((None, d, bn), lambda i, j: (layer, 0, j)),
        ],
        out_specs=pl.BlockSpec((bm, bn), lambda i, j: (i, j)),
        out_shape=jax.ShapeDtypeStruct((m, n), BF16),
        compiler_params=_params("parallel", "parallel"),
        name="gate_up",
    )(h, wg, wu)


def _mixer_a_kernel(slope_ref, sink_ref, q_ref, kp_ref, kc_ref, kn_ref, vp_ref, vc_ref, vn_ref, o_ref,
                    *, tq, seq):
    i = pl.program_id(1)
    g = pl.program_id(2)
    nsub = tq // A_BLOCK
    kcat = jnp.concatenate([kp_ref[...], kc_ref[...], kn_ref[...]], axis=0)
    vcat = jnp.concatenate([vp_ref[...], vc_ref[...], vn_ref[...]], axis=0)
    qi = lax.broadcasted_iota(jnp.int32, (A_BLOCK, 3 * A_BLOCK), 0)
    kj = lax.broadcasted_iota(jnp.int32, (A_BLOCK, 3 * A_BLOCK), 1)
    rel = kj - A_BLOCK - qi
    dist = jnp.abs(rel)
    dist_f = dist.astype(F32)
    in_window = dist <= A_WINDOW
    scale = 1.0 / math.sqrt(HEAD_DIM)
    units = [(j, r) for j in range(nsub) for r in range(A_REP)]
    scores = []
    for j, r in units:
        kband = kcat[j * A_BLOCK:(j + 3) * A_BLOCK]
        q = q_ref[j * A_BLOCK:(j + 1) * A_BLOCK, r * HEAD_DIM:(r + 1) * HEAD_DIM]
        scores.append(lax.dot_general(q, kband, (((1,), (1,)), ((), ())), preferred_element_type=F32))
    probs, inv_denoms = [], []
    for (j, r), s in zip(units, scores):
        head = g * A_REP + r
        sink = sink_ref[head] * LOG2E
        kpos = (i * nsub + j - 1) * A_BLOCK + kj
        valid = in_window & (kpos >= 0) & (kpos < seq)
        s = s * (scale * LOG2E) - (slope_ref[head] * LOG2E) * dist_f
        s = jnp.where(valid, s, NEG_INF)
        mx = jnp.maximum(jnp.max(s, axis=-1, keepdims=True), sink)
        p = jnp.exp2(s - mx)
        denom = jnp.sum(p, axis=-1, keepdims=True) + jnp.exp2(sink - mx)
        probs.append(p.astype(BF16))
        inv_denoms.append(1.0 / denom)
    for (j, r), p, inv in zip(units, probs, inv_denoms):
        vband = vcat[j * A_BLOCK:(j + 3) * A_BLOCK]
        o = jnp.dot(p, vband, preferred_element_type=F32) * inv
        o_ref[j * A_BLOCK:(j + 1) * A_BLOCK, r * HEAD_DIM:(r + 1) * HEAD_DIM] = o.astype(o_ref.dtype)


def mixer_a(qkv, sink, batch, seq, tq=512):
    tq = min(tq, seq)
    nq = seq // tq
    nsub = tq // A_BLOCK
    nblk = seq // A_BLOCK
    kcol = A_Q_HEADS
    vcol = A_Q_HEADS + A_KV_HEADS
    slopes = 2.0 ** (-8.0 * jnp.arange(1, A_Q_HEADS + 1, dtype=F32) / A_Q_HEADS)

    def cur(col):
        return pl.BlockSpec((tq, HEAD_DIM), lambda b, i, g: (b * nq + i, col + g))

    def prev(col):
        return pl.BlockSpec((A_BLOCK, HEAD_DIM),
                            lambda b, i, g: (b * nblk + jnp.maximum(i * nsub - 1, 0), col + g))

    def nxt(col):
        return pl.BlockSpec((A_BLOCK, HEAD_DIM),
                            lambda b, i, g: (b * nblk + jnp.minimum((i + 1) * nsub, nblk - 1), col + g))

    smem = pl.BlockSpec(memory_space=pltpu.SMEM)
    return pl.pallas_call(
        functools.partial(_mixer_a_kernel, tq=tq, seq=seq),
        grid=(batch, nq, A_KV_HEADS),
        in_specs=[smem, smem,
                  pl.BlockSpec((tq, A_REP * HEAD_DIM), lambda b, i, g: (b * nq + i, g)),
                  prev(kcol), cur(kcol), nxt(kcol), prev(vcol), cur(vcol), nxt(vcol)],
        out_specs=pl.BlockSpec((tq, A_REP * HEAD_DIM), lambda b, i, g: (b * nq + i, g)),
        out_shape=jax.ShapeDtypeStruct((batch * seq, A_Q_HEADS * HEAD_DIM), BF16),
        compiler_params=_params("parallel", "parallel", "parallel"),
        name="mixer_a",
    )(slopes, sink.astype(F32), qkv, qkv, qkv, qkv, qkv, qkv, qkv)


def _mixer_b_kernel(q_ref, k_ref, v_ref, bias_ref, o_ref, *, rows, wr, group):
    scale = 1.0 / math.sqrt(HEAD_DIM)
    nkeys = wr * GRID_W

    def row_slice(r, n):
        return pl.ds(pl.multiple_of(r * GRID_W, GRID_W), n)

    def body(gi, carry):
        rws = [gi * group + j for j in range(group)]
        starts = [jnp.clip(r - wr // 2, 0, rows - wr) for r in rws]
        scores = [lax.dot_general(q_ref[row_slice(r, GRID_W), :], k_ref[row_slice(rs, nkeys), :],
                                  (((1,), (1,)), ((), ())), preferred_element_type=F32)
                  for r, rs in zip(rws, starts)]
        probs, inv_denoms = [], []
        for r, rs, s in zip(rws, starts, scores):
            s = s * (scale * LOG2E) + bias_ref[0, r - rs]
            mx = jnp.max(s, axis=-1, keepdims=True)
            p = jnp.exp2(s - mx)
            probs.append(p.astype(BF16))
            inv_denoms.append(1.0 / jnp.sum(p, axis=-1, keepdims=True))
        for r, rs, p, inv in zip(rws, starts, probs, inv_denoms):
            o = jnp.dot(p, v_ref[row_slice(rs, nkeys), :], preferred_element_type=F32) * inv
            o_ref[row_slice(r, GRID_W), :] = o.astype(o_ref.dtype)
        return carry

    lax.fori_loop(0, rows // group, body, 0)


def _mixer_b_bias(rpb, wr):
    qc = np.arange(GRID_W)[:, None]
    kc = np.arange(GRID_W)[None, :]
    cstart = np.clip(qc - B_WIN_COLS // 2, 0, GRID_W - B_WIN_COLS)
    col_ok = (kc >= cstart) & (kc < cstart + B_WIN_COLS)
    lo = GRID_W - B_WIN_COLS
    tbl = jnp.pad(rpb.astype(F32), ((0, 0), (0, 0), (0, 0), (lo, lo)), mode="edge")
    b = jnp.stack([tbl[..., GRID_W - 1 - q:2 * GRID_W - 1 - q] for q in range(GRID_W)], axis=-2)
    b = jnp.stack([b[:, :, B_WIN_ROWS - 1 - v:B_WIN_ROWS - 1 - v + wr] for v in range(wr)], axis=2)
    b = jnp.where(jnp.asarray(col_ok), b * LOG2E, NEG_INF)
    b = b.transpose(0, 1, 2, 4, 3, 5)
    return b.reshape(rpb.shape[0], rpb.shape[1], wr, GRID_W, wr * GRID_W)


def mixer_b(qkv, bias, batch, seq):
    rows = seq // GRID_W
    wr = min(B_WIN_ROWS, rows)
    group = math.gcd(rows, B_ROW_GROUP)

    def col(off):
        return pl.BlockSpec((seq, HEAD_DIM), lambda b, h: (b, off + h))

    return pl.pallas_call(
        functools.partial(_mixer_b_kernel, rows=rows, wr=wr, group=group),
        grid=(batch, B_HEADS),
        in_specs=[col(0), col(B_HEADS), col(2 * B_HEADS),
                  pl.BlockSpec((1, wr, GRID_W, wr * GRID_W), lambda b, h: (h, 0, 0, 0))],
        out_specs=pl.BlockSpec((seq, HEAD_DIM), lambda b, h: (b, h)),
        out_shape=jax.ShapeDtypeStruct((batch * seq, B_HEADS * HEAD_DIM), BF16),
        compiler_params=_params("parallel", "parallel"),
        name="mixer_b",
    )(qkv, qkv, qkv, bias)


C_QW = 2 * HEAD_DIM
C_ONES_ROWS = 16
C_KV_TILE = 2048
C_LOOKAHEAD = 2
_NT = (((1,), (1,)), ((), ()))


def _rms_bf16(c, g):
    inv = lax.rsqrt(jnp.mean(c * c, axis=-1, keepdims=True) + NORM_EPS)
    return (c * inv * g).astype(BF16)


def _mla_q_kernel(c_ref, g_ref, w_ref, t_ref, o_ref, *, scale):
    a = _rms_bf16(c_ref[...], g_ref[...])
    table = t_ref[...]
    for h in range(C_HEADS):
        r0 = h * C_QW
        acc = lax.dot_general(w_ref[r0:r0 + C_QW, :], a, _NT, preferred_element_type=F32)
        o_ref[r0:r0 + C_NOPE, :] = (acc[:C_NOPE] * scale).astype(o_ref.dtype)
        t = acc[C_NOPE:] * table
        rope = (t[:C_ROPE] + t[C_ROPE:]) * scale
        o_ref[r0 + C_NOPE:r0 + C_NOPE + C_ROPE, :] = rope.astype(o_ref.dtype)
        o_ref[r0 + C_NOPE + C_ROPE:r0 + C_QW, :] = jnp.zeros((C_ROPE, rope.shape[1]), o_ref.dtype)


def mla_q_proj(c_all, q_norm, wq_t, layer, table_t, seq, bm=512):
    m = c_all.shape[0]
    bm = _tile(min(m, seq), bm)
    nt = seq // bm
    scale = LOG2E / math.sqrt(C_NOPE + C_ROPE)
    return pl.pallas_call(
        functools.partial(_mla_q_kernel, scale=scale),
        grid=(m // bm,),
        in_specs=[pl.BlockSpec((bm, C_Q_RANK), lambda i: (i, 0)),
                  pl.BlockSpec((1, C_Q_RANK), lambda i: (0, 0)),
                  pl.BlockSpec((None, C_HEADS * C_QW, C_Q_RANK), lambda i: (layer, 0, 0)),
                  pl.BlockSpec((LANES, bm), lambda i: (0, i % nt))],
        out_specs=pl.BlockSpec((C_HEADS * C_QW, bm), lambda i: (0, i)),
        out_shape=jax.ShapeDtypeStruct((C_HEADS * C_QW, m), BF16),
        compiler_params=_params("parallel"),
        name="mla_q_proj",
    )(c_all, q_norm.reshape(1, C_Q_RANK).astype(F32), wq_t, table_t)


def _mla_kv_kernel(c_ref, g_ref, wk_ref, wvt_ref, kr_ref, t_ref, ok_ref, ovt_ref, okr_ref):
    a = _rms_bf16(c_ref[...], g_ref[...])
    ok_ref[...] = jnp.dot(a, wk_ref[...], preferred_element_type=F32).astype(ok_ref.dtype)
    ovt_ref[0] = lax.dot_general(wvt_ref[...], a, _NT, preferred_element_type=F32).astype(ovt_ref.dtype)
    t = kr_ref[...] * t_ref[...]
    rope = t + pltpu.roll(t, C_ROPE, axis=1)
    lane = lax.broadcasted_iota(jnp.int32, rope.shape, 1)
    okr_ref[...] = jnp.where(lane < C_ROPE, rope, 0.0).astype(okr_ref.dtype)


def mla_kv_proj(c_all, kv_norm, wk, wv_t, layer, table, seq, bm):
    m = c_all.shape[0]
    nt = seq // bm
    n = wk.shape[2]
    ckv_blk = C_Q_RANK // C_KV_RANK
    kr_blk = (C_Q_RANK + C_KV_RANK) // LANES
    return pl.pallas_call(
        _mla_kv_kernel,
        grid=(m // bm,),
        in_specs=[pl.BlockSpec((bm, C_KV_RANK), lambda i: (i, ckv_blk)),
                  pl.BlockSpec((1, C_KV_RANK), lambda i: (0, 0)),
                  pl.BlockSpec((None, C_KV_RANK, n), lambda i: (layer, 0, 0)),
                  pl.BlockSpec((None, n, C_KV_RANK), lambda i: (layer, 0, 0)),
                  pl.BlockSpec((bm, LANES), lambda i: (i, kr_blk)),
                  pl.BlockSpec((bm, LANES), lambda i: (i % nt, 0))],
        out_specs=[pl.BlockSpec((bm, n), lambda i: (i, 0)),
                   pl.BlockSpec((1, n, bm), lambda i: (i, 0, 0)),
                   pl.BlockSpec((bm, LANES), lambda i: (i, 0))],
        out_shape=[jax.ShapeDtypeStruct((m, n), BF16),
                   jax.ShapeDtypeStruct((m // bm, n, bm), BF16),
                   jax.ShapeDtypeStruct((m, LANES), BF16)],
        compiler_params=_params("parallel"),
        name="mla_kv_proj",
    )(c_all, kv_norm.reshape(1, C_KV_RANK).astype(F32), wk, wv_t, c_all, table)


def _mla_attn_kernel(qt_ref, kn_ref, kr_ref, vt_ref, o_ref, s_ref, m_ref, acc_ref, *, bk, nkv, cw):
    bq = qt_ref.shape[1]
    cols = [slice(c * cw, (c + 1) * cw) for c in range(bq // cw)]
    m_ref[...] = jnp.full(m_ref.shape, NEG_INF, F32)
    acc_ref[...] = jnp.zeros(acc_ref.shape, F32)
    ones = jnp.ones((C_ONES_ROWS, bk), BF16)

    def keys(t):
        off = pl.multiple_of(t * bk, bk)
        return jnp.concatenate([kn_ref[pl.ds(off, bk), :], kr_ref[pl.ds(off, bk), :]], axis=1)

    ncb = len(cols)

    def scores(k, c):
        return jnp.dot(k, qt_ref[:, cols[c]], preferred_element_type=F32)

    def softmax_update(slot, cs):
        s = s_ref[slot]
        m_old = m_ref[:, cs]
        m_new = jnp.maximum(m_old, jnp.max(s, axis=0, keepdims=True))
        m_ref[:, cs] = m_new
        return jnp.exp2(s - m_new).astype(BF16), jnp.exp2(m_old - m_new)

    def step(t, last):
        vt = jnp.concatenate([vt_ref[t], ones], axis=0)
        k_cur = keys(t)
        k_nxt = None if last else keys(t + 1)
        for c, cs in enumerate(cols):
            p, alpha = softmax_update(c, cs)
            ahead = c + C_LOOKAHEAD
            if ahead < ncb:
                s_ref[ahead] = scores(k_cur, ahead)
            elif not last:
                s_ref[ahead - ncb] = scores(k_nxt, ahead - ncb)
            acc_ref[:, cs] = alpha * acc_ref[:, cs] + jnp.dot(vt, p, preferred_element_type=F32)

    k0 = keys(0)
    for c in range(C_LOOKAHEAD):
        s_ref[c] = scores(k0, c)

    def body(t, carry):
        step(t, False)
        return carry

    lax.fori_loop(0, nkv - 1, body, 0)
    step(nkv - 1, True)
    out_t = acc_ref[:C_V, :] / acc_ref[C_V:C_V + 1, :]
    o_ref[...] = out_t.T.astype(o_ref.dtype)


def mla_attention(q_t, kn, kr, v_t, batch, seq, bk, bq=1024, cw=256):
    bq = min(bq, seq)
    cw = min(cw, bq)
    nq = seq // bq
    nkv = seq // bk
    return pl.pallas_call(
        functools.partial(_mla_attn_kernel, bk=bk, nkv=nkv, cw=cw),
        grid=(batch, C_HEADS, nq),
        in_specs=[pl.BlockSpec((C_QW, bq), lambda b, h, i: (h, b * nq + i)),
                  pl.BlockSpec((seq, C_NOPE), lambda b, h, i: (b, h)),
                  pl.BlockSpec((seq, LANES), lambda b, h, i: (b, 0)),
                  pl.BlockSpec((nkv, C_V, bk), lambda b, h, i: (b, h, 0))],
        out_specs=pl.BlockSpec((bq, C_V), lambda b, h, i: (b * nq + i, h)),
        out_shape=jax.ShapeDtypeStruct((batch * seq, C_HEADS * C_V), BF16),
        scratch_shapes=[pltpu.VMEM((bq // cw, bk, cw), F32), pltpu.VMEM((1, bq), F32),
                        pltpu.VMEM((C_V + C_ONES_ROWS, bq), F32)],
        compiler_params=_params("parallel", "parallel", "arbitrary"),
        name="mla_attention",
    )(q_t, kn, kr, v_t)


def _xattn_kernel(x_ref, g_ref, wq_ref, k_ref, v_ref, wo_ref, g2_ref, xo_ref, ho_ref, o_scr):
    x = x_ref[...]
    h = _rms_bf16(x, g_ref[...])
    q = jnp.dot(h, wq_ref[...], preferred_element_type=F32).astype(BF16)
    scale = 1.0 / math.sqrt(HEAD_DIM)
    heads = [slice(hh * HEAD_DIM, (hh + 1) * HEAD_DIM) for hh in range(X_HEADS)]
    scores = [lax.dot_general(q[:, sl], k_ref[:, sl], (((1,), (1,)), ((), ())), preferred_element_type=F32)
              for sl in heads]
    probs, inv_denoms = [], []
    for s in scores:
        s = s * (scale * LOG2E)
        mx = jnp.max(s, axis=-1, keepdims=True)
        p = jnp.exp2(s - mx)
        probs.append(p.astype(BF16))
        inv_denoms.append(1.0 / jnp.sum(p, axis=-1, keepdims=True))
    for sl, p, inv in zip(heads, probs, inv_denoms):
        o = jnp.dot(p, v_ref[:, sl], preferred_element_type=F32) * inv
        o_scr[:, sl] = o.astype(o_scr.dtype)
    x_new = x + jnp.dot(o_scr[...], wo_ref[...], preferred_element_type=F32)
    xo_ref[...] = x_new
    ho_ref[...] = _rms_bf16(x_new, g2_ref[...])


def xattn_block(x, g, wq, kv_mem, wo, g_next, layer, seq, n_mem, bm=512):
    m, d = x.shape
    bm = _tile(min(m, seq), bm)
    per_b = seq // bm
    xd = X_HEADS * HEAD_DIM
    once = pl.Buffered(1)
    return pl.pallas_call(
        _xattn_kernel,
        grid=(m // bm,),
        in_specs=[pl.BlockSpec((bm, d), lambda i: (i, 0)),
                  pl.BlockSpec((1, d), lambda i: (0, 0)),
                  pl.BlockSpec((None, d, xd), lambda i: (layer, 0, 0), pipeline_mode=once),
                  pl.BlockSpec((n_mem, xd), lambda i: (i // per_b, 0)),
                  pl.BlockSpec((n_mem, xd), lambda i: (i // per_b, 1)),
                  pl.BlockSpec((None, xd, d), lambda i: (layer, 0, 0), pipeline_mode=once),
                  pl.BlockSpec((1, d), lambda i: (0, 0))],
        out_specs=[pl.BlockSpec((bm, d), lambda i: (i, 0)), pl.BlockSpec((bm, d), lambda i: (i, 0))],
        out_shape=[jax.ShapeDtypeStruct((m, d), F32), jax.ShapeDtypeStruct((m, d), BF16)],
        scratch_shapes=[pltpu.VMEM((bm, xd), BF16)],
        compiler_params=_params("parallel"),
        name="xattn_block",
    )(x, g.reshape(1, d).astype(F32), wq, kv_mem, kv_mem, wo, g_next.reshape(1, d).astype(F32))


def _rot_cols(w):
    half = w.shape[-1] // 2
    return jnp.concatenate([-w[..., half:], w[..., :half]], axis=-1)


A_IN_W = (A_Q_HEADS + 2 * A_KV_HEADS) * HEAD_DIM
B_IN_W = 3 * B_HEADS * HEAD_DIM


def _prep_weights(w_in, c_w_q_b, c_w_kv_b, w_out, x_w_q, x_w_k, x_w_v, x_w_o, w_gate, w_up, w_down, ff_pad):
    nl = w_in.shape[0]
    c_off = A_IN_W + B_IN_W
    w_in = w_in.astype(BF16)
    w_ckr = w_in[:, :, c_off + C_Q_RANK + C_KV_RANK:]
    w_c = jnp.concatenate([w_in[:, :, c_off:], _rot_cols(w_ckr)], axis=2)

    wq = c_w_q_b.reshape(nl, C_Q_RANK, C_HEADS, C_NOPE + C_ROPE)
    wq = jnp.concatenate([wq, _rot_cols(wq[..., C_NOPE:])], axis=-1)
    wq_t = wq.reshape(nl, C_Q_RANK, C_HEADS * C_QW).astype(BF16).transpose(0, 2, 1)

    wkv = c_w_kv_b.reshape(nl, C_KV_RANK, C_HEADS, C_NOPE + C_V).astype(BF16)
    wk = wkv[..., :C_NOPE].reshape(nl, C_KV_RANK, C_HEADS * C_NOPE)
    wv_t = wkv[..., C_NOPE:].reshape(nl, C_KV_RANK, C_HEADS * C_V).transpose(0, 2, 1)

    return dict(w_in=w_in, w_c=w_c, wq_t=wq_t, wk=wk, wv_t=wv_t, w_out=w_out.astype(BF16),
                wg=cast_pad_cols(w_gate, ff_pad), wu=cast_pad_cols(w_up, ff_pad), wd=cast_pad_rows(w_down, ff_pad),
                x_w_q=x_w_q.astype(BF16), w_xkv=jnp.concatenate([x_w_k, x_w_v], axis=2).astype(BF16),
                x_w_o=x_w_o.astype(BF16))


def _rope_table(seq):
    inv = 1.0 / (ROPE_BASE ** (jnp.arange(0, C_ROPE, 2, dtype=F32) / C_ROPE))
    ang = jnp.arange(seq, dtype=F32)[:, None] * inv[None, :]
    cos, sin = jnp.cos(ang), jnp.sin(ang)
    return jnp.concatenate([cos, cos, sin, sin], axis=1)


def kernel(x, mem, ln_mix, w_in, a_sink, b_rpb, c_q_norm, c_w_q_b, c_kv_norm, c_w_kv_b, w_out,
           ln_xattn, ln_mem, x_w_q, x_w_k, x_w_v, x_w_o, ln_ffn, w_gate, w_up, w_down, ln_final):
    batch, seq, d = x.shape
    n_mem = mem.shape[1]
    depth = w_in.shape[0]
    ff = w_gate.shape[2]
    ff_pad = -(-ff // 1024) * 1024
    table = _rope_table(seq)
    table_t = table.T
    kv_tile = min(C_KV_TILE, seq)
    xs = x.reshape(batch * seq, d)
    mem2 = mem.reshape(batch * n_mem, d)
    w = _prep_weights(w_in, c_w_q_b, c_w_kv_b, w_out, x_w_q, x_w_k, x_w_v, x_w_o, w_gate, w_up, w_down, ff_pad)
    b_bias = _mixer_b_bias(b_rpb, min(B_WIN_ROWS, seq // GRID_W))
    for l in range(depth):
        h = rmsnorm(xs, ln_mix[l], BF16)
        qkv_a = matmul(h, w["w_in"], l, BF16, n=A_IN_W, bn=1280, name="w_in_a")
        qkv_b = matmul(h, w["w_in"], l, BF16, n=B_IN_W, col0=A_IN_W, name="w_in_b")
        c_all = matmul(h, w["w_c"], l, F32, bm=512, bn=w["w_c"].shape[2], name="w_in_c")
        ya = mixer_a(qkv_a, a_sink[l], batch, seq)
        yb = mixer_b(qkv_b, b_bias[l], batch, seq)
        q_t = mla_q_proj(c_all, c_q_norm[l], w["wq_t"], l, table_t, seq)
        kn_c, v_t, kr_c = mla_kv_proj(c_all, c_kv_norm[l], w["wk"], w["wv_t"], l, table, seq, kv_tile)
        yc = mla_attention(q_t, kn_c, kr_c, v_t, batch, seq, kv_tile)
        xs = matmul([ya, yb, yc], w["w_out"], l, F32, res=xs, name="w_out")
        mem_n = rmsnorm(mem2, ln_mem[l], BF16)
        kv_mem = matmul(mem_n, w["w_xkv"], l, BF16, name="xattn_kv")
        xs, h = xattn_block(xs, ln_xattn[l], w["x_w_q"], kv_mem, w["x_w_o"], ln_ffn[l], l, seq, n_mem)
        act = gate_up(h, w["wg"], w["wu"], l)
        xs = matmul(act, w["wd"], l, F32, res=xs, bk=ff_pad // 4, name="ffn_down")
    out = rmsnorm(xs, ln_final, F32)
    return out.reshape(batch, seq, d)
```

```python
import functools
import math

import numpy as np
import jax
import jax.numpy as jnp
from jax import lax
from jax.experimental import pallas as pl
from jax.experimental.pallas import tpu as pltpu

F32 = jnp.float32
BF16 = jnp.bfloat16

HEAD_DIM = 128
GRID_W = 64
A_Q_HEADS = 12
A_KV_HEADS = 4
A_REP = A_Q_HEADS // A_KV_HEADS
A_WINDOW = 128
A_BLOCK = 128
B_HEADS = 8
B_WIN_ROWS = 8
B_WIN_COLS = 16
B_ROW_GROUP = 16
C_HEADS = 12
C_Q_RANK = 1024
C_KV_RANK = 512
C_NOPE = 128
C_ROPE = 64
C_V = 128
ROPE_BASE = 10000.0
X_HEADS = 4
NORM_EPS = 1e-6
NEG_INF = -1e30
LOG2E = 1.4426950408889634

VMEM_LIMIT_BYTES = 58 * 1024 * 1024
LANES = 128


def _params(*sem):
    return pltpu.CompilerParams(dimension_semantics=sem, vmem_limit_bytes=VMEM_LIMIT_BYTES)


def _tile(n, pref):
    if n <= pref:
        return n
    t = (pref // LANES) * LANES
    while t >= LANES:
        if n % t == 0:
            return t
        t -= LANES
    return n


def _rmsnorm_kernel(x_ref, g_ref, o_ref):
    x = x_ref[...].astype(F32)
    inv = lax.rsqrt(jnp.mean(x * x, axis=-1, keepdims=True) + NORM_EPS)
    o_ref[...] = (x * inv * g_ref[...]).astype(o_ref.dtype)


def rmsnorm(x, g, out_dtype, bm=256):
    m, d = x.shape
    bm = min(bm, m)
    return pl.pallas_call(
        _rmsnorm_kernel,
        grid=(m // bm,),
        in_specs=[pl.BlockSpec((bm, d), lambda i: (i, 0)), pl.BlockSpec((1, d), lambda i: (0, 0))],
        out_specs=pl.BlockSpec((bm, d), lambda i: (i, 0)),
        out_shape=jax.ShapeDtypeStruct((m, d), out_dtype),
        compiler_params=_params("parallel"),
        name="rmsnorm",
    )(x, g.reshape(1, d).astype(F32))


def _cast_pad_cols_kernel(x_ref, o_ref):
    n = x_ref.shape[-1]
    o_ref[:, :n] = x_ref[...].astype(o_ref.dtype)
    if o_ref.shape[1] > n:
        o_ref[:, n:] = jnp.zeros((o_ref.shape[0], o_ref.shape[1] - n), o_ref.dtype)


def cast_pad_cols(w, n_pad, br=256):
    nl, r, n = w.shape
    br = _tile(r, br)
    return pl.pallas_call(
        _cast_pad_cols_kernel,
        grid=(nl, r // br),
        in_specs=[pl.BlockSpec((None, br, n), lambda l, i: (l, i, 0))],
        out_specs=pl.BlockSpec((None, br, n_pad), lambda l, i: (l, i, 0)),
        out_shape=jax.ShapeDtypeStruct((nl, r, n_pad), BF16),
        compiler_params=_params("parallel", "parallel"),
        name="cast_pad_cols",
    )(w)


def _cast_pad_rows_kernel(x_ref, o_ref, *, n_src_blocks):
    i = pl.program_id(1)

    @pl.when(i < n_src_blocks)
    def _():
        o_ref[...] = x_ref[...].astype(o_ref.dtype)

    @pl.when(i >= n_src_blocks)
    def _():
        o_ref[...] = jnp.zeros(o_ref.shape, o_ref.dtype)


def cast_pad_rows(w, r_pad, br=256):
    nl, r, n = w.shape
    br = math.gcd(math.gcd(r, r_pad), br)
    nsrc = r // br
    return pl.pallas_call(
        functools.partial(_cast_pad_rows_kernel, n_src_blocks=nsrc),
        grid=(nl, r_pad // br),
        in_specs=[pl.BlockSpec((None, br, n), lambda l, i: (l, jnp.minimum(i, nsrc - 1), 0))],
        out_specs=pl.BlockSpec((None, br, n), lambda l, i: (l, i, 0)),
        out_shape=jax.ShapeDtypeStruct((nl, r_pad, n), BF16),
        compiler_params=_params("parallel", "parallel"),
        name="cast_pad_rows",
    )(w)


def _mm_kernel(*refs, n_a, nk, has_res):
    a_refs = refs[:n_a]
    w_ref = refs[n_a]
    pos = n_a + 1
    r_ref = refs[pos] if has_res else None
    pos += int(has_res)
    o_ref = refs[pos]

    if nk > 1:
        @pl.when(pl.program_id(2) == 0)
        def _():
            o_ref[...] = r_ref[...] if has_res else jnp.zeros(o_ref.shape, o_ref.dtype)

    a = a_refs[0][...] if n_a == 1 else jnp.concatenate([r[...] for r in a_refs], axis=1)
    part = jnp.dot(a, w_ref[...], preferred_element_type=F32)
    if nk > 1:
        o_ref[...] += part
    else:
        if has_res:
            part = r_ref[...] + part
        o_ref[...] = part.astype(o_ref.dtype)


def matmul(a_list, w, layer, out_dtype, *, n=None, col0=0, res=None, bm=1024, bn=1024, bk=None, name="matmul"):
    if not isinstance(a_list, (list, tuple)):
        a_list = [a_list]
    m = a_list[0].shape[0]
    kdim = sum(a.shape[1] for a in a_list)
    assert w.shape[1] == kdim
    n = w.shape[2] - col0 if n is None else n
    bm = _tile(m, bm)
    bn = _tile(math.gcd(n, col0) if col0 else n, bn)
    assert n % bn == 0 and col0 % bn == 0
    n_a = len(a_list)
    if bk is None or n_a > 1:
        nk = 1
    else:
        assert kdim % bk == 0
        nk = kdim // bk
    in_specs = []
    for a in a_list:
        kd = a.shape[1] if nk == 1 else bk
        in_specs.append(pl.BlockSpec((bm, kd), lambda i, j, k: (i, k)))
    cb0 = col0 // bn
    in_specs.append(pl.BlockSpec((None, kdim if nk == 1 else bk, bn), lambda i, j, k: (layer, k, cb0 + j)))
    args = list(a_list) + [w]
    if res is not None:
        in_specs.append(pl.BlockSpec((bm, bn), lambda i, j, k: (i, j)))
        args.append(res)
    assert nk == 1 or out_dtype == F32
    return pl.pallas_call(
        functools.partial(_mm_kernel, n_a=n_a, nk=nk, has_res=res is not None),
        grid=(m // bm, n // bn, nk),
        in_specs=in_specs,
        out_specs=pl.BlockSpec((bm, bn), lambda i, j, k: (i, j)),
        out_shape=jax.ShapeDtypeStruct((m, n), out_dtype),
        compiler_params=_params("parallel", "parallel", "arbitrary"),
        name=name,
    )(*args)


def _norm_proj_kernel(x_ref, g_ref, w_ref, h_ref, o_ref):
    x = x_ref[...]
    inv = lax.rsqrt(jnp.mean(x * x, axis=-1, keepdims=True) + NORM_EPS)
    h = (x * inv * g_ref[...]).astype(h_ref.dtype)
    h_ref[...] = h
    o_ref[...] = jnp.dot(h, w_ref[...], preferred_element_type=F32)


def norm_proj(x, g, w, layer, bm=512):
    m, d = x.shape
    n = w.shape[2]
    bm = _tile(m, bm)
    return pl.pallas_call(
        _norm_proj_kernel,
        grid=(m // bm,),
        in_specs=[pl.BlockSpec((bm, d), lambda i: (i, 0)),
                  pl.BlockSpec((1, d), lambda i: (0, 0)),
                  pl.BlockSpec((None, d, n), lambda i: (layer, 0, 0), pipeline_mode=pl.Buffered(1))],
        out_specs=[pl.BlockSpec((bm, d), lambda i: (i, 0)), pl.BlockSpec((bm, n), lambda i: (i, 0))],
        out_shape=[jax.ShapeDtypeStruct((m, d), BF16), jax.ShapeDtypeStruct((m, n), F32)],
        compiler_params=_params("parallel"),
        name="norm_proj",
    )(x, g.reshape(1, d).astype(F32), w)


def _gate_up_kernel(h_ref, wg_ref, wu_ref, o_ref):
    h = h_ref[...]
    g = jnp.dot(h, wg_ref[...], preferred_element_type=F32)
    u = jnp.dot(h, wu_ref[...], preferred_element_type=F32)
    o_ref[...] = (jax.nn.silu(g) * u).astype(o_ref.dtype)


def gate_up(h, wg, wu, layer, bm=1024, bn=512):
    m, d = h.shape
    n = wg.shape[2]
    bm = _tile(m, bm)
    bn = _tile(n, bn)
    return pl.pallas_call(
        _gate_up_kernel,
        grid=(m // bm, n // bn),
        in_specs=[
            pl.BlockSpec((bm, d), lambda i, j: (i, 0)),
            pl.BlockSpec((None, d, bn), lambda i, j: (layer, 0, j)),
            pl.BlockSpec((None, d, bn), lambda i, j: (layer, 0, j)),
        ],
        out_specs=pl.BlockSpec((bm, bn), lambda i, j: (i, j)),
        out_shape=jax.ShapeDtypeStruct((m, n), BF16),
        compiler_params=_params("parallel", "parallel"),
        name="gate_up",
    )(h, wg, wu)


def _mixer_a_kernel(slope_ref, sink_ref, q_ref, kp_ref, kc_ref, kn_ref, vp_ref, vc_ref, vn_ref, o_ref,
                    *, tq, seq):
    i = pl.program_id(1)
    g = pl.program_id(2)
    nsub = tq // A_BLOCK
    kcat = jnp.concatenate([kp_ref[...], kc_ref[...], kn_ref[...]], axis=0)
    vcat = jnp.concatenate([vp_ref[...], vc_ref[...], vn_ref[...]], axis=0)
    qi = lax.broadcasted_iota(jnp.int32, (A_BLOCK, 3 * A_BLOCK), 0)
    kj = lax.broadcasted_iota(jnp.int32, (A_BLOCK, 3 * A_BLOCK), 1)
    rel = kj - A_BLOCK - qi
    dist = jnp.abs(rel)
    dist_f = dist.astype(F32)
    in_window = dist <= A_WINDOW
    scale = 1.0 / math.sqrt(HEAD_DIM)
    units = [(j, r) for j in range(nsub) for r in range(A_REP)]
    scores = []
    for j, r in units:
        kband = kcat[j * A_BLOCK:(j + 3) * A_BLOCK]
        q = q_ref[j * A_BLOCK:(j + 1) * A_BLOCK, r * HEAD_DIM:(r + 1) * HEAD_DIM]
        scores.append(lax.dot_general(q, kband, (((1,), (1,)), ((), ())), preferred_element_type=F32))
    probs, inv_denoms = [], []
    for (j, r), s in zip(units, scores):
        head = g * A_REP + r
        sink = sink_ref[head] * LOG2E
        kpos = (i * nsub + j - 1) * A_BLOCK + kj
        valid = in_window & (kpos >= 0) & (kpos < seq)
        s = s * (scale * LOG2E) - (slope_ref[head] * LOG2E) * dist_f
        s = jnp.where(valid, s, NEG_INF)
        mx = jnp.maximum(jnp.max(s, axis=-1, keepdims=True), sink)
        p = jnp.exp2(s - mx)
        denom = jnp.sum(p, axis=-1, keepdims=True) + jnp.exp2(sink - mx)
        probs.append(p.astype(BF16))
        inv_denoms.append(1.0 / denom)
    for (j, r), p, inv in zip(units, probs, inv_denoms):
        vband = vcat[j * A_BLOCK:(j + 3) * A_BLOCK]
        o = jnp.dot(p, vband, preferred_element_type=F32) * inv
        o_ref[j * A_BLOCK:(j + 1) * A_BLOCK, r * HEAD_DIM:(r + 1) * HEAD_DIM] = o.astype(o_ref.dtype)


def mixer_a(qkv, sink, batch, seq, tq=512):
    tq = min(tq, seq)
    nq = seq // tq
    nsub = tq // A_BLOCK
    nblk = seq // A_BLOCK
    kcol = A_Q_HEADS
    vcol = A_Q_HEADS + A_KV_HEADS
    slopes = 2.0 ** (-8.0 * jnp.arange(1, A_Q_HEADS + 1, dtype=F32) / A_Q_HEADS)

    def cur(col):
        return pl.BlockSpec((tq, HEAD_DIM), lambda b, i, g: (b * nq + i, col + g))

    def prev(col):
        return pl.BlockSpec((A_BLOCK, HEAD_DIM),
                            lambda b, i, g: (b * nblk + jnp.maximum(i * nsub - 1, 0), col + g))

    def nxt(col):
        return pl.BlockSpec((A_BLOCK, HEAD_DIM),
                            lambda b, i, g: (b * nblk + jnp.minimum((i + 1) * nsub, nblk - 1), col + g))

    smem = pl.BlockSpec(memory_space=pltpu.SMEM)
    return pl.pallas_call(
        functools.partial(_mixer_a_kernel, tq=tq, seq=seq),
        grid=(batch, nq, A_KV_HEADS),
        in_specs=[smem, smem,
                  pl.BlockSpec((tq, A_REP * HEAD_DIM), lambda b, i, g: (b * nq + i, g)),
                  prev(kcol), cur(kcol), nxt(kcol), prev(vcol), cur(vcol), nxt(vcol)],
        out_specs=pl.BlockSpec((tq, A_REP * HEAD_DIM), lambda b, i, g: (b * nq + i, g)),
        out_shape=jax.ShapeDtypeStruct((batch * seq, A_Q_HEADS * HEAD_DIM), BF16),
        compiler_params=_params("parallel", "parallel", "parallel"),
        name="mixer_a",
    )(slopes, sink.astype(F32), qkv, qkv, qkv, qkv, qkv, qkv, qkv)


def _mixer_b_kernel(q_ref, k_ref, v_ref, bias_ref, o_ref, *, rows, wr, group):
    scale = 1.0 / math.sqrt(HEAD_DIM)
    nkeys = wr * GRID_W

    def row_slice(r, n):
        return pl.ds(pl.multiple_of(r * GRID_W, GRID_W), n)

    def body(gi, carry):
        rws = [gi * group + j for j in range(group)]
        starts = [jnp.clip(r - wr // 2, 0, rows - wr) for r in rws]
        scores = [lax.dot_general(q_ref[row_slice(r, GRID_W), :], k_ref[row_slice(rs, nkeys), :],
                                  (((1,), (1,)), ((), ())), preferred_element_type=F32)
                  for r, rs in zip(rws, starts)]
        probs, inv_denoms = [], []
        for r, rs, s in zip(rws, starts, scores):
            s = s * (scale * LOG2E) + bias_ref[0, r - rs]
            mx = jnp.max(s, axis=-1, keepdims=True)
            p = jnp.exp2(s - mx)
            probs.append(p.astype(BF16))
            inv_denoms.append(1.0 / jnp.sum(p, axis=-1, keepdims=True))
        for r, rs, p, inv in zip(rws, starts, probs, inv_denoms):
            o = jnp.dot(p, v_ref[row_slice(rs, nkeys), :], preferred_element_type=F32) * inv
            o_ref[row_slice(r, GRID_W), :] = o.astype(o_ref.dtype)
        return carry

    lax.fori_loop(0, rows // group, body, 0)


def _mixer_b_bias(rpb, wr):
    qc = np.arange(GRID_W)[:, None]
    kc = np.arange(GRID_W)[None, :]
    cstart = np.clip(qc - B_WIN_COLS // 2, 0, GRID_W - B_WIN_COLS)
    col_ok = (kc >= cstart) & (kc < cstart + B_WIN_COLS)
    lo = GRID_W - B_WIN_COLS
    tbl = jnp.pad(rpb.astype(F32), ((0, 0), (0, 0), (0, 0), (lo, lo)), mode="edge")
    b = jnp.stack([tbl[..., GRID_W - 1 - q:2 * GRID_W - 1 - q] for q in range(GRID_W)], axis=-2)
    b = jnp.stack([b[:, :, B_WIN_ROWS - 1 - v:B_WIN_ROWS - 1 - v + wr] for v in range(wr)], axis=2)
    b = jnp.where(jnp.asarray(col_ok), b * LOG2E, NEG_INF)
    b = b.transpose(0, 1, 2, 4, 3, 5)
    return b.reshape(rpb.shape[0], rpb.shape[1], wr, GRID_W, wr * GRID_W)


def mixer_b(qkv, bias, batch, seq):
    rows = seq // GRID_W
    wr = min(B_WIN_ROWS, rows)
    group = math.gcd(rows, B_ROW_GROUP)

    def col(off):
        return pl.BlockSpec((seq, HEAD_DIM), lambda b, h: (b, off + h))

    return pl.pallas_call(
        functools.partial(_mixer_b_kernel, rows=rows, wr=wr, group=group),
        grid=(batch, B_HEADS),
        in_specs=[col(0), col(B_HEADS), col(2 * B_HEADS),
                  pl.BlockSpec((1, wr, GRID_W, wr * GRID_W), lambda b, h: (h, 0, 0, 0))],
        out_specs=pl.BlockSpec((seq, HEAD_DIM), lambda b, h: (b, h)),
        out_shape=jax.ShapeDtypeStruct((batch * seq, B_HEADS * HEAD_DIM), BF16),
        compiler_params=_params("parallel", "parallel"),
        name="mixer_b",
    )(qkv, qkv, qkv, bias)


C_QW = 2 * HEAD_DIM
C_ONES_ROWS = 16
C_KV_TILE = 2048
C_LOOKAHEAD = 2
_NT = (((1,), (1,)), ((), ()))


def _rms_bf16(c, g):
    inv = lax.rsqrt(jnp.mean(c * c, axis=-1, keepdims=True) + NORM_EPS)
    return (c * inv * g).astype(BF16)


def _mla_q_kernel(c_ref, g_ref, w_ref, t_ref, o_ref, *, scale):
    a = _rms_bf16(c_ref[...], g_ref[...])
    table = t_ref[...]
    for h in range(C_HEADS):
        r0 = h * C_QW
        acc = lax.dot_general(w_ref[r0:r0 + C_QW, :], a, _NT, preferred_element_type=F32)
        o_ref[r0:r0 + C_NOPE, :] = (acc[:C_NOPE] * scale).astype(o_ref.dtype)
        t = acc[C_NOPE:] * table
        rope = (t[:C_ROPE] + t[C_ROPE:]) * scale
        o_ref[r0 + C_NOPE:r0 + C_NOPE + C_ROPE, :] = rope.astype(o_ref.dtype)
        o_ref[r0 + C_NOPE + C_ROPE:r0 + C_QW, :] = jnp.zeros((C_ROPE, rope.shape[1]), o_ref.dtype)


def mla_q_proj(c_all, q_norm, wq_t, layer, table_t, seq, bm=512):
    m = c_all.shape[0]
    bm = _tile(min(m, seq), bm)
    nt = seq // bm
    scale = LOG2E / math.sqrt(C_NOPE + C_ROPE)
    return pl.pallas_call(
        functools.partial(_mla_q_kernel, scale=scale),
        grid=(m // bm,),
        in_specs=[pl.BlockSpec((bm, C_Q_RANK), lambda i: (i, 0)),
                  pl.BlockSpec((1, C_Q_RANK), lambda i: (0, 0)),
                  pl.BlockSpec((None, C_HEADS * C_QW, C_Q_RANK), lambda i: (layer, 0, 0)),
                  pl.BlockSpec((LANES, bm), lambda i: (0, i % nt))],
        out_specs=pl.BlockSpec((C_HEADS * C_QW, bm), lambda i: (0, i)),
        out_shape=jax.ShapeDtypeStruct((C_HEADS * C_QW, m), BF16),
        compiler_params=_params("parallel"),
        name="mla_q_proj",
    )(c_all, q_norm.reshape(1, C_Q_RANK).astype(F32), wq_t, table_t)


def _mla_kv_kernel(c_ref, g_ref, wk_ref, wvt_ref, kr_ref, t_ref, ok_ref, ovt_ref, okr_ref):
    a = _rms_bf16(c_ref[...], g_ref[...])
    ok_ref[...] = jnp.dot(a, wk_ref[...], preferred_element_type=F32).astype(ok_ref.dtype)
    ovt_ref[0] = lax.dot_general(wvt_ref[...], a, _NT, preferred_element_type=F32).astype(ovt_ref.dtype)
    t = kr_ref[...] * t_ref[...]
    rope = t + pltpu.roll(t, C_ROPE, axis=1)
    lane = lax.broadcasted_iota(jnp.int32, rope.shape, 1)
    okr_ref[...] = jnp.where(lane < C_ROPE, rope, 0.0).astype(okr_ref.dtype)


def mla_kv_proj(c_all, kv_norm, wk, wv_t, layer, table, seq, bm):
    m = c_all.shape[0]
    nt = seq // bm
    n = wk.shape[2]
    ckv_blk = C_Q_RANK // C_KV_RANK
    kr_blk = (C_Q_RANK + C_KV_RANK) // LANES
    return pl.pallas_call(
        _mla_kv_kernel,
        grid=(m // bm,),
        in_specs=[pl.BlockSpec((bm, C_KV_RANK), lambda i: (i, ckv_blk)),
                  pl.BlockSpec((1, C_KV_RANK), lambda i: (0, 0)),
                  pl.BlockSpec((None, C_KV_RANK, n), lambda i: (layer, 0, 0)),
                  pl.BlockSpec((None, n, C_KV_RANK), lambda i: (layer, 0, 0)),
                  pl.BlockSpec((bm, LANES), lambda i: (i, kr_blk)),
                  pl.BlockSpec((bm, LANES), lambda i: (i % nt, 0))],
        out_specs=[pl.BlockSpec((bm, n), lambda i: (i, 0)),
                   pl.BlockSpec((1, n, bm), lambda i: (i, 0, 0)),
                   pl.BlockSpec((bm, LANES), lambda i: (i, 0))],
        out_shape=[jax.ShapeDtypeStruct((m, n), BF16),
                   jax.ShapeDtypeStruct((m // bm, n, bm), BF16),
                   jax.ShapeDtypeStruct((m, LANES), BF16)],
        compiler_params=_params("parallel"),
        name="mla_kv_proj",
    )(c_all, kv_norm.reshape(1, C_KV_RANK).astype(F32), wk, wv_t, c_all, table)


def _mla_attn_kernel(qt_ref, kn_ref, kr_ref, vt_ref, o_ref, s_ref, m_ref, acc_ref, *, bk, nkv, cw):
    bq = qt_ref.shape[1]
    cols = [slice(c * cw, (c + 1) * cw) for c in range(bq // cw)]
    m_ref[...] = jnp.full(m_ref.shape, NEG_INF, F32)
    acc_ref[...] = jnp.zeros(acc_ref.shape, F32)
    ones = jnp.ones((C_ONES_ROWS, bk), BF16)

    def keys(t):
        off = pl.multiple_of(t * bk, bk)
        return jnp.concatenate([kn_ref[pl.ds(off, bk), :], kr_ref[pl.ds(off, bk), :]], axis=1)

    ncb = len(cols)

    def scores(k, c):
        return jnp.dot(k, qt_ref[:, cols[c]], preferred_element_type=F32)

    def softmax_update(slot, cs):
        s = s_ref[slot]
        m_old = m_ref[:, cs]
        m_new = jnp.maximum(m_old, jnp.max(s, axis=0, keepdims=True))
        m_ref[:, cs] = m_new
        return jnp.exp2(s - m_new).astype(BF16), jnp.exp2(m_old - m_new)

    def step(t, last):
        vt = jnp.concatenate([vt_ref[t], ones], axis=0)
        k_cur = keys(t)
        k_nxt = None if last else keys(t + 1)
        for c, cs in enumerate(cols):
            p, alpha = softmax_update(c, cs)
            ahead = c + C_LOOKAHEAD
            if ahead < ncb:
                s_ref[ahead] = scores(k_cur, ahead)
            elif not last:
                s_ref[ahead - ncb] = scores(k_nxt, ahead - ncb)
            acc_ref[:, cs] = alpha * acc_ref[:, cs] + jnp.dot(vt, p, preferred_element_type=F32)

    k0 = keys(0)
    for c in range(C_LOOKAHEAD):
        s_ref[c] = scores(k0, c)

    def body(t, carry):
        step(t, False)
        return carry

    lax.fori_loop(0, nkv - 1, body, 0)
    step(nkv - 1, True)
    out_t = acc_ref[:C_V, :] / acc_ref[C_V:C_V + 1, :]
    o_ref[...] = out_t.T.astype(o_ref.dtype)


def mla_attention(q_t, kn, kr, v_t, batch, seq, bk, bq=1024, cw=256):
    bq = min(bq, seq)
    cw = min(cw, bq)
    nq = seq // bq
    nkv = seq // bk
    return pl.pallas_call(
        functools.partial(_mla_attn_kernel, bk=bk, nkv=nkv, cw=cw),
        grid=(batch, C_HEADS, nq),
        in_specs=[pl.BlockSpec((C_QW, bq), lambda b, h, i: (h, b * nq + i)),
                  pl.BlockSpec((seq, C_NOPE), lambda b, h, i: (b, h)),
                  pl.BlockSpec((seq, LANES), lambda b, h, i: (b, 0)),
                  pl.BlockSpec((nkv, C_V, bk), lambda b, h, i: (b, h, 0))],
        out_specs=pl.BlockSpec((bq, C_V), lambda b, h, i: (b * nq + i, h)),
        out_shape=jax.ShapeDtypeStruct((batch * seq, C_HEADS * C_V), BF16),
        scratch_shapes=[pltpu.VMEM((bq // cw, bk, cw), F32), pltpu.VMEM((1, bq), F32),
                        pltpu.VMEM((C_V + C_ONES_ROWS, bq), F32)],
        compiler_params=_params("parallel", "parallel", "arbitrary"),
        name="mla_attention",
    )(q_t, kn, kr, v_t)


def _xattn_kernel(x_ref, g_ref, wq_ref, k_ref, v_ref, wo_ref, g2_ref, xo_ref, ho_ref, o_scr):
    x = x_ref[...]
    h = _rms_bf16(x, g_ref[...])
    q = jnp.dot(h, wq_ref[...], preferred_element_type=F32).astype(BF16)
    scale = 1.0 / math.sqrt(HEAD_DIM)
    heads = [slice(hh * HEAD_DIM, (hh + 1) * HEAD_DIM) for hh in range(X_HEADS)]
    scores = [lax.dot_general(q[:, sl], k_ref[:, sl], (((1,), (1,)), ((), ())), preferred_element_type=F32)
              for sl in heads]
    probs, inv_denoms = [], []
    for s in scores:
        s = s * (scale * LOG2E)
        mx = jnp.max(s, axis=-1, keepdims=True)
        p = jnp.exp2(s - mx)
        probs.append(p.astype(BF16))
        inv_denoms.append(1.0 / jnp.sum(p, axis=-1, keepdims=True))
    for sl, p, inv in zip(heads, probs, inv_denoms):
        o = jnp.dot(p, v_ref[:, sl], preferred_element_type=F32) * inv
        o_scr[:, sl] = o.astype(o_scr.dtype)
    x_new = x + jnp.dot(o_scr[...], wo_ref[...], preferred_element_type=F32)
    xo_ref[...] = x_new
    ho_ref[...] = _rms_bf16(x_new, g2_ref[...])


def xattn_block(x, g, wq, kv_mem, wo, g_next, layer, seq, n_mem, bm=512):
    m, d = x.shape
    bm = _tile(min(m, seq), bm)
    per_b = seq // bm
    xd = X_HEADS * HEAD_DIM
    once = pl.Buffered(1)
    return pl.pallas_call(
        _xattn_kernel,
        grid=(m // bm,),
        in_specs=[pl.BlockSpec((bm, d), lambda i: (i, 0)),
                  pl.BlockSpec((1, d), lambda i: (0, 0)),
                  pl.BlockSpec((None, d, xd), lambda i: (layer, 0, 0), pipeline_mode=once),
                  pl.BlockSpec((n_mem, xd), lambda i: (i // per_b, 0)),
                  pl.BlockSpec((n_mem, xd), lambda i: (i // per_b, 1)),
                  pl.BlockSpec((None, xd, d), lambda i: (layer, 0, 0), pipeline_mode=once),
                  pl.BlockSpec((1, d), lambda i: (0, 0))],
        out_specs=[pl.BlockSpec((bm, d), lambda i: (i, 0)), pl.BlockSpec((bm, d), lambda i: (i, 0))],
        out_shape=[jax.ShapeDtypeStruct((m, d), F32), jax.ShapeDtypeStruct((m, d), BF16)],
        scratch_shapes=[pltpu.VMEM((bm, xd), BF16)],
        compiler_params=_params("parallel"),
        name="xattn_block",
    )(x, g.reshape(1, d).astype(F32), wq, kv_mem, kv_mem, wo, g_next.reshape(1, d).astype(F32))


def _rot_cols(w):
    half = w.shape[-1] // 2
    return jnp.concatenate([-w[..., half:], w[..., :half]], axis=-1)


A_IN_W = (A_Q_HEADS + 2 * A_KV_HEADS) * HEAD_DIM
B_IN_W = 3 * B_HEADS * HEAD_DIM


def _prep_weights(w_in, c_w_q_b, c_w_kv_b, w_out, x_w_q, x_w_k, x_w_v, x_w_o, w_gate, w_up, w_down, ff_pad):
    nl = w_in.shape[0]
    c_off = A_IN_W + B_IN_W
    w_in = cast_pad_cols(w_in, w_in.shape[2])
    w_ckr = w_in[:, :, c_off + C_Q_RANK + C_KV_RANK:]
    w_c = jnp.concatenate([w_in[:, :, c_off:], _rot_cols(w_ckr)], axis=2)

    wq = c_w_q_b.reshape(nl, C_Q_RANK, C_HEADS, C_NOPE + C_ROPE)
    wq = jnp.concatenate([wq, _rot_cols(wq[..., C_NOPE:])], axis=-1)
    wq_t = wq.reshape(nl, C_Q_RANK, C_HEADS * C_QW).astype(BF16).transpose(0, 2, 1)

    wkv = c_w_kv_b.reshape(nl, C_KV_RANK, C_HEADS, C_NOPE + C_V).astype(BF16)
    wk = wkv[..., :C_NOPE].reshape(nl, C_KV_RANK, C_HEADS * C_NOPE)
    wv_t = wkv[..., C_NOPE:].reshape(nl, C_KV_RANK, C_HEADS * C_V).transpose(0, 2, 1)

    return dict(w_in=w_in, w_c=w_c, wq_t=wq_t, wk=wk, wv_t=wv_t, w_out=w_out.astype(BF16),
                wg=cast_pad_cols(w_gate, ff_pad), wu=cast_pad_cols(w_up, ff_pad), wd=cast_pad_rows(w_down, ff_pad),
                x_w_q=x_w_q.astype(BF16), w_xkv=jnp.concatenate([x_w_k, x_w_v], axis=2).astype(BF16),
                x_w_o=x_w_o.astype(BF16))


def _rope_table(seq):
    inv = 1.0 / (ROPE_BASE ** (jnp.arange(0, C_ROPE, 2, dtype=F32) / C_ROPE))
    ang = jnp.arange(seq, dtype=F32)[:, None] * inv[None, :]
    cos, sin = jnp.cos(ang), jnp.sin(ang)
    return jnp.concatenate([cos, cos, sin, sin], axis=1)


def kernel(x, mem, ln_mix, w_in, a_sink, b_rpb, c_q_norm, c_w_q_b, c_kv_norm, c_w_kv_b, w_out,
           ln_xattn, ln_mem, x_w_q, x_w_k, x_w_v, x_w_o, ln_ffn, w_gate, w_up, w_down, ln_final):
    batch, seq, d = x.shape
    n_mem = mem.shape[1]
    depth = w_in.shape[0]
    ff = w_gate.shape[2]
    ff_pad = -(-ff // 1024) * 1024
    table = _rope_table(seq)
    table_t = table.T
    kv_tile = min(C_KV_TILE, seq)
    xs = x.reshape(batch * seq, d)
    mem2 = mem.reshape(batch * n_mem, d)
    w = _prep_weights(w_in, c_w_q_b, c_w_kv_b, w_out, x_w_q, x_w_k, x_w_v, x_w_o, w_gate, w_up, w_down, ff_pad)
    b_bias = _mixer_b_bias(b_rpb, min(B_WIN_ROWS, seq // GRID_W))
    for l in range(depth):
        h, c_all = norm_proj(xs, ln_mix[l], w["w_c"], l)
        qkv_a = matmul(h, w["w_in"], l, BF16, n=A_IN_W, bn=1280, name="w_in_a")
        qkv_b = matmul(h, w["w_in"], l, BF16, n=B_IN_W, col0=A_IN_W, name="w_in_b")
        ya = mixer_a(qkv_a, a_sink[l], batch, seq)
        yb = mixer_b(qkv_b, b_bias[l], batch, seq)
        q_t = mla_q_proj(c_all, c_q_norm[l], w["wq_t"], l, table_t, seq)
        kn_c, v_t, kr_c = mla_kv_proj(c_all, c_kv_norm[l], w["wk"], w["wv_t"], l, table, seq, kv_tile)
        yc = mla_attention(q_t, kn_c, kr_c, v_t, batch, seq, kv_tile)
        xs = matmul([ya, yb, yc], w["w_out"], l, F32, res=xs, name="w_out")
        mem_n = rmsnorm(mem2, ln_mem[l], BF16)
        kv_mem = matmul(mem_n, w["w_xkv"], l, BF16, name="xattn_kv")
        xs, h = xattn_block(xs, ln_xattn[l], w["x_w_q"], kv_mem, w["x_w_o"], ln_ffn[l], l, seq, n_mem)
        act = gate_up(h, w["wg"], w["wu"], l)
        xs = matmul(act, w["wd"], l, F32, res=xs, bk=ff_pad // 4, name="ffn_down")
    out = rmsnorm(xs, ln_final, F32)
    return out.reshape(batch, seq, d)
```

```python
import functools
import math

import numpy as np
import jax
import jax.numpy as jnp
from jax import lax
from jax.experimental import pallas as pl
from jax.experimental.pallas import tpu as pltpu

F32 = jnp.float32
BF16 = jnp.bfloat16

HEAD_DIM = 128
GRID_W = 64
A_Q_HEADS = 12
A_KV_HEADS = 4
A_REP = A_Q_HEADS // A_KV_HEADS
A_WINDOW = 128
A_BLOCK = 128
B_HEADS = 8
B_WIN_ROWS = 8
B_WIN_COLS = 16
B_ROW_GROUP = 16
C_HEADS = 12
C_Q_RANK = 1024
C_KV_RANK = 512
C_NOPE = 128
C_ROPE = 64
C_V = 128
ROPE_BASE = 10000.0
X_HEADS = 4
NORM_EPS = 1e-6
NEG_INF = -1e30
LOG2E = 1.4426950408889634

VMEM_LIMIT_BYTES = 58 * 1024 * 1024
LANES = 128


def _params(*sem):
    return pltpu.CompilerParams(dimension_semantics=sem, vmem_limit_bytes=VMEM_LIMIT_BYTES)


def _tile(n, pref):
    if n <= pref:
        return n
    t = (pref // LANES) * LANES
    while t >= LANES:
        if n % t == 0:
            return t
        t -= LANES
    return n


def _rmsnorm_kernel(x_ref, g_ref, o_ref):
    x = x_ref[...].astype(F32)
    inv = lax.rsqrt(jnp.mean(x * x, axis=-1, keepdims=True) + NORM_EPS)
    o_ref[...] = (x * inv * g_ref[...]).astype(o_ref.dtype)


def rmsnorm(x, g, out_dtype, bm=256):
    m, d = x.shape
    bm = min(bm, m)
    return pl.pallas_call(
        _rmsnorm_kernel,
        grid=(m // bm,),
        in_specs=[pl.BlockSpec((bm, d), lambda i: (i, 0)), pl.BlockSpec((1, d), lambda i: (0, 0))],
        out_specs=pl.BlockSpec((bm, d), lambda i: (i, 0)),
        out_shape=jax.ShapeDtypeStruct((m, d), out_dtype),
        compiler_params=_params("parallel"),
        name="rmsnorm",
    )(x, g.reshape(1, d).astype(F32))


def _cast_pad_cols_kernel(x_ref, o_ref):
    n = x_ref.shape[-1]
    o_ref[:, :n] = x_ref[...].astype(o_ref.dtype)
    if o_ref.shape[1] > n:
        o_ref[:, n:] = jnp.zeros((o_ref.shape[0], o_ref.shape[1] - n), o_ref.dtype)


def cast_pad_cols(w, n_pad, br=256):
    nl, r, n = w.shape
    br = _tile(r, br)
    return pl.pallas_call(
        _cast_pad_cols_kernel,
        grid=(nl, r // br),
        in_specs=[pl.BlockSpec((None, br, n), lambda l, i: (l, i, 0))],
        out_specs=pl.BlockSpec((None, br, n_pad), lambda l, i: (l, i, 0)),
        out_shape=jax.ShapeDtypeStruct((nl, r, n_pad), BF16),
        compiler_params=_params("parallel", "parallel"),
        name="cast_pad_cols",
    )(w)


def _cast_pad_rows_kernel(x_ref, o_ref, *, n_src_blocks):
    i = pl.program_id(1)

    @pl.when(i < n_src_blocks)
    def _():
        o_ref[...] = x_ref[...].astype(o_ref.dtype)

    @pl.when(i >= n_src_blocks)
    def _():
        o_ref[...] = jnp.zeros(o_ref.shape, o_ref.dtype)


def cast_pad_rows(w, r_pad, br=256):
    nl, r, n = w.shape
    br = math.gcd(math.gcd(r, r_pad), br)
    nsrc = r // br
    return pl.pallas_call(
        functools.partial(_cast_pad_rows_kernel, n_src_blocks=nsrc),
        grid=(nl, r_pad // br),
        in_specs=[pl.BlockSpec((None, br, n), lambda l, i: (l, jnp.minimum(i, nsrc - 1), 0))],
        out_specs=pl.BlockSpec((None, br, n), lambda l, i: (l, i, 0)),
        out_shape=jax.ShapeDtypeStruct((nl, r_pad, n), BF16),
        compiler_params=_params("parallel", "parallel"),
        name="cast_pad_rows",
    )(w)


def _cast_transposed_kernel(x_ref, o_ref):
    o_ref[...] = x_ref[...].T.astype(o_ref.dtype)


def cast_transposed(w_t, n_rows, br=512):
    nl, _, k = w_t.shape
    br = _tile(n_rows, br)
    return pl.pallas_call(
        _cast_transposed_kernel,
        grid=(nl, n_rows // br),
        in_specs=[pl.BlockSpec((None, br, k), lambda l, j: (l, j, 0))],
        out_specs=pl.BlockSpec((None, k, br), lambda l, j: (l, 0, j)),
        out_shape=jax.ShapeDtypeStruct((nl, k, n_rows), BF16),
        compiler_params=_params("parallel", "parallel"),
        name="cast_transposed",
    )(w_t)


def _mm_kernel(*refs, n_a, nk, has_res):
    a_refs = refs[:n_a]
    w_ref = refs[n_a]
    pos = n_a + 1
    r_ref = refs[pos] if has_res else None
    pos += int(has_res)
    o_ref = refs[pos]

    if nk > 1:
        @pl.when(pl.program_id(2) == 0)
        def _():
            o_ref[...] = r_ref[...] if has_res else jnp.zeros(o_ref.shape, o_ref.dtype)

    a = a_refs[0][...] if n_a == 1 else jnp.concatenate([r[...] for r in a_refs], axis=1)
    part = jnp.dot(a, w_ref[...], preferred_element_type=F32)
    if nk > 1:
        o_ref[...] += part
    else:
        if has_res:
            part = r_ref[...] + part
        o_ref[...] = part.astype(o_ref.dtype)


def matmul(a_list, w, layer, out_dtype, *, n=None, col0=0, res=None, bm=1024, bn=1024, bk=None, name="matmul"):
    if not isinstance(a_list, (list, tuple)):
        a_list = [a_list]
    m = a_list[0].shape[0]
    kdim = sum(a.shape[1] for a in a_list)
    assert w.shape[1] == kdim
    n = w.shape[2] - col0 if n is None else n
    bm = _tile(m, bm)
    bn = _tile(math.gcd(n, col0) if col0 else n, bn)
    assert n % bn == 0 and col0 % bn == 0
    n_a = len(a_list)
    if bk is None or n_a > 1:
        nk = 1
    else:
        assert kdim % bk == 0
        nk = kdim // bk
    in_specs = []
    for a in a_list:
        kd = a.shape[1] if nk == 1 else bk
        in_specs.append(pl.BlockSpec((bm, kd), lambda i, j, k: (i, k)))
    cb0 = col0 // bn
    in_specs.append(pl.BlockSpec((None, kdim if nk == 1 else bk, bn), lambda i, j, k: (layer, k, cb0 + j)))
    args = list(a_list) + [w]
    if res is not None:
        in_specs.append(pl.BlockSpec((bm, bn), lambda i, j, k: (i, j)))
        args.append(res)
    assert nk == 1 or out_dtype == F32
    return pl.pallas_call(
        functools.partial(_mm_kernel, n_a=n_a, nk=nk, has_res=res is not None),
        grid=(m // bm, n // bn, nk),
        in_specs=in_specs,
        out_specs=pl.BlockSpec((bm, bn), lambda i, j, k: (i, j)),
        out_shape=jax.ShapeDtypeStruct((m, n), out_dtype),
        compiler_params=_params("parallel", "parallel", "arbitrary"),
        name=name,
    )(*args)


def _norm_proj_kernel(x_ref, g_ref, w_ref, h_ref, o_ref):
    x = x_ref[...]
    inv = lax.rsqrt(jnp.mean(x * x, axis=-1, keepdims=True) + NORM_EPS)
    h = (x * inv * g_ref[...]).astype(h_ref.dtype)
    h_ref[...] = h
    o_ref[...] = jnp.dot(h, w_ref[...], preferred_element_type=F32)


def norm_proj(x, g, w, layer, bm=512):
    m, d = x.shape
    n = w.shape[2]
    bm = _tile(m, bm)
    return pl.pallas_call(
        _norm_proj_kernel,
        grid=(m // bm,),
        in_specs=[pl.BlockSpec((bm, d), lambda i: (i, 0)),
                  pl.BlockSpec((1, d), lambda i: (0, 0)),
                  pl.BlockSpec((None, d, n), lambda i: (layer, 0, 0), pipeline_mode=pl.Buffered(1))],
        out_specs=[pl.BlockSpec((bm, d), lambda i: (i, 0)), pl.BlockSpec((bm, n), lambda i: (i, 0))],
        out_shape=[jax.ShapeDtypeStruct((m, d), BF16), jax.ShapeDtypeStruct((m, n), F32)],
        compiler_params=_params("parallel"),
        name="norm_proj",
    )(x, g.reshape(1, d).astype(F32), w)


def _gate_up_kernel(h_ref, wg_ref, wu_ref, o_ref):
    h = h_ref[...]
    g = jnp.dot(h, wg_ref[...], preferred_element_type=F32)
    u = jnp.dot(h, wu_ref[...], preferred_element_type=F32)
    o_ref[...] = (jax.nn.silu(g) * u).astype(o_ref.dtype)


def gate_up(h, wg, wu, layer, bm=1024, bn=512):
    m, d = h.shape
    n = wg.shape[2]
    bm = _tile(m, bm)
    bn = _tile(n, bn)
    return pl.pallas_call(
        _gate_up_kernel,
        grid=(m // bm, n // bn),
        in_specs=[
            pl.BlockSpec((bm, d), lambda i, j: (i, 0)),
            pl.BlockSpec((None, d, bn), lambda i, j: (layer, 0, j)),
            pl.BlockSpec((None, d, bn), lambda i, j: (layer, 0, j)),
        ],
        out_specs=pl.BlockSpec((bm, bn), lambda i, j: (i, j)),
        out_shape=jax.ShapeDtypeStruct((m, n), BF16),
        compiler_params=_params("parallel", "parallel"),
        name="gate_up",
    )(h, wg, wu)


def _mixer_a_kernel(slope_ref, sink_ref, q_ref, kp_ref, kc_ref, kn_ref, vp_ref, vc_ref, vn_ref, o_ref,
                    *, tq, seq):
    i = pl.program_id(1)
    g = pl.program_id(2)
    nsub = tq // A_BLOCK
    kcat = jnp.concatenate([kp_ref[...], kc_ref[...], kn_ref[...]], axis=0)
    vcat = jnp.concatenate([vp_ref[...], vc_ref[...], vn_ref[...]], axis=0)
    qi = lax.broadcasted_iota(jnp.int32, (A_BLOCK, 3 * A_BLOCK), 0)
    kj = lax.broadcasted_iota(jnp.int32, (A_BLOCK, 3 * A_BLOCK), 1)
    rel = kj - A_BLOCK - qi
    dist = jnp.abs(rel)
    dist_f = dist.astype(F32)
    in_window = dist <= A_WINDOW
    scale = 1.0 / math.sqrt(HEAD_DIM)
    units = [(j, r) for j in range(nsub) for r in range(A_REP)]
    scores = []
    for j, r in units:
        kband = kcat[j * A_BLOCK:(j + 3) * A_BLOCK]
        q = q_ref[j * A_BLOCK:(j + 1) * A_BLOCK, r * HEAD_DIM:(r + 1) * HEAD_DIM]
        scores.append(lax.dot_general(q, kband, (((1,), (1,)), ((), ())), preferred_element_type=F32))
    alibi = [jnp.where(in_window, -(slope_ref[g * A_REP + r] * LOG2E) * dist_f, NEG_INF) for r in range(A_REP)]
    in_seq = {}
    for j in {0, nsub - 1}:
        kpos = (i * nsub + j - 1) * A_BLOCK + kj
        in_seq[j] = (kpos >= 0) & (kpos < seq)
    probs, inv_denoms = [], []
    for (j, r), s in zip(units, scores):
        sink = sink_ref[g * A_REP + r] * LOG2E
        s = s * (scale * LOG2E) + alibi[r]
        if j in in_seq:
            s = jnp.where(in_seq[j], s, NEG_INF)
        mx = jnp.maximum(jnp.max(s, axis=-1, keepdims=True), sink)
        p = jnp.exp2(s - mx)
        denom = jnp.sum(p, axis=-1, keepdims=True) + jnp.exp2(sink - mx)
        probs.append(p.astype(BF16))
        inv_denoms.append(1.0 / denom)
    for (j, r), p, inv in zip(units, probs, inv_denoms):
        vband = vcat[j * A_BLOCK:(j + 3) * A_BLOCK]
        o = jnp.dot(p, vband, preferred_element_type=F32) * inv
        o_ref[j * A_BLOCK:(j + 1) * A_BLOCK, r * HEAD_DIM:(r + 1) * HEAD_DIM] = o.astype(o_ref.dtype)


def mixer_a(qkv, sink, batch, seq, tq=512):
    tq = min(tq, seq)
    nq = seq // tq
    nsub = tq // A_BLOCK
    nblk = seq // A_BLOCK
    kcol = A_Q_HEADS
    vcol = A_Q_HEADS + A_KV_HEADS
    slopes = 2.0 ** (-8.0 * jnp.arange(1, A_Q_HEADS + 1, dtype=F32) / A_Q_HEADS)

    def cur(col):
        return pl.BlockSpec((tq, HEAD_DIM), lambda b, i, g: (b * nq + i, col + g))

    def prev(col):
        return pl.BlockSpec((A_BLOCK, HEAD_DIM),
                            lambda b, i, g: (b * nblk + jnp.maximum(i * nsub - 1, 0), col + g))

    def nxt(col):
        return pl.BlockSpec((A_BLOCK, HEAD_DIM),
                            lambda b, i, g: (b * nblk + jnp.minimum((i + 1) * nsub, nblk - 1), col + g))

    smem = pl.BlockSpec(memory_space=pltpu.SMEM)
    return pl.pallas_call(
        functools.partial(_mixer_a_kernel, tq=tq, seq=seq),
        grid=(batch, nq, A_KV_HEADS),
        in_specs=[smem, smem,
                  pl.BlockSpec((tq, A_REP * HEAD_DIM), lambda b, i, g: (b * nq + i, g)),
                  prev(kcol), cur(kcol), nxt(kcol), prev(vcol), cur(vcol), nxt(vcol)],
        out_specs=pl.BlockSpec((tq, A_REP * HEAD_DIM), lambda b, i, g: (b * nq + i, g)),
        out_shape=jax.ShapeDtypeStruct((batch * seq, A_Q_HEADS * HEAD_DIM), BF16),
        compiler_params=_params("parallel", "parallel", "parallel"),
        name="mixer_a",
    )(slopes, sink.astype(F32), qkv, qkv, qkv, qkv, qkv, qkv, qkv)


def _mixer_b_kernel(q_ref, k_ref, v_ref, bias_ref, o_ref, *, rows, wr, group):
    scale = 1.0 / math.sqrt(HEAD_DIM)
    nkeys = wr * GRID_W

    def row_slice(r, n):
        return pl.ds(pl.multiple_of(r * GRID_W, GRID_W), n)

    def body(gi, carry):
        rws = [gi * group + j for j in range(group)]
        starts = [jnp.clip(r - wr // 2, 0, rows - wr) for r in rws]
        scores = [lax.dot_general(q_ref[row_slice(r, GRID_W), :], k_ref[row_slice(rs, nkeys), :],
                                  (((1,), (1,)), ((), ())), preferred_element_type=F32)
                  for r, rs in zip(rws, starts)]
        probs, inv_denoms = [], []
        for r, rs, s in zip(rws, starts, scores):
            s = s * (scale * LOG2E) + bias_ref[0, r - rs]
            mx = jnp.max(s, axis=-1, keepdims=True)
            p = jnp.exp2(s - mx)
            probs.append(p.astype(BF16))
            inv_denoms.append(1.0 / jnp.sum(p, axis=-1, keepdims=True))
        for r, rs, p, inv in zip(rws, starts, probs, inv_denoms):
            o = jnp.dot(p, v_ref[row_slice(rs, nkeys), :], preferred_element_type=F32) * inv
            o_ref[row_slice(r, GRID_W), :] = o.astype(o_ref.dtype)
        return carry

    lax.fori_loop(0, rows // group, body, 0)


def _mixer_b_bias(rpb, wr):
    qc = np.arange(GRID_W)[:, None]
    kc = np.arange(GRID_W)[None, :]
    cstart = np.clip(qc - B_WIN_COLS // 2, 0, GRID_W - B_WIN_COLS)
    col_ok = (kc >= cstart) & (kc < cstart + B_WIN_COLS)
    lo = GRID_W - B_WIN_COLS
    tbl = jnp.pad(rpb.astype(F32), ((0, 0), (0, 0), (0, 0), (lo, lo)), mode="edge")
    b = jnp.stack([tbl[..., GRID_W - 1 - q:2 * GRID_W - 1 - q] for q in range(GRID_W)], axis=-2)
    b = jnp.stack([b[:, :, B_WIN_ROWS - 1 - v:B_WIN_ROWS - 1 - v + wr] for v in range(wr)], axis=2)
    b = jnp.where(jnp.asarray(col_ok), b * LOG2E, NEG_INF)
    b = b.transpose(0, 1, 2, 4, 3, 5)
    return b.reshape(rpb.shape[0], rpb.shape[1], wr, GRID_W, wr * GRID_W)


def mixer_b(qkv, bias, batch, seq):
    rows = seq // GRID_W
    wr = min(B_WIN_ROWS, rows)
    group = math.gcd(rows, B_ROW_GROUP)

    def col(off):
        return pl.BlockSpec((seq, HEAD_DIM), lambda b, h: (b, off + h))

    return pl.pallas_call(
        functools.partial(_mixer_b_kernel, rows=rows, wr=wr, group=group),
        grid=(batch, B_HEADS),
        in_specs=[col(0), col(B_HEADS), col(2 * B_HEADS),
                  pl.BlockSpec((1, wr, GRID_W, wr * GRID_W), lambda b, h: (h, 0, 0, 0))],
        out_specs=pl.BlockSpec((seq, HEAD_DIM), lambda b, h: (b, h)),
        out_shape=jax.ShapeDtypeStruct((batch * seq, B_HEADS * HEAD_DIM), BF16),
        compiler_params=_params("parallel", "parallel"),
        name="mixer_b",
    )(qkv, qkv, qkv, bias)


C_QW = 2 * HEAD_DIM
C_ONES_ROWS = 16
C_KV_TILE = 2048
C_LOOKAHEAD = 2
_NT = (((1,), (1,)), ((), ()))


def _rms_bf16(c, g):
    inv = lax.rsqrt(jnp.mean(c * c, axis=-1, keepdims=True) + NORM_EPS)
    return (c * inv * g).astype(BF16)


def _mla_q_kernel(c_ref, g_ref, w_ref, t_ref, o_ref, *, scale):
    a = _rms_bf16(c_ref[...], g_ref[...])
    table = t_ref[...]
    for h in range(C_HEADS):
        r0 = h * C_QW
        acc = lax.dot_general(w_ref[r0:r0 + C_QW, :], a, _NT, preferred_element_type=F32)
        o_ref[r0:r0 + C_NOPE, :] = (acc[:C_NOPE] * scale).astype(o_ref.dtype)
        t = acc[C_NOPE:] * table
        rope = (t[:C_ROPE] + t[C_ROPE:]) * scale
        o_ref[r0 + C_NOPE:r0 + C_NOPE + C_ROPE, :] = rope.astype(o_ref.dtype)
        o_ref[r0 + C_NOPE + C_ROPE:r0 + C_QW, :] = jnp.zeros((C_ROPE, rope.shape[1]), o_ref.dtype)


def mla_q_proj(c_all, q_norm, wq_t, layer, table_t, seq, bm=512):
    m = c_all.shape[0]
    bm = _tile(min(m, seq), bm)
    nt = seq // bm
    scale = LOG2E / math.sqrt(C_NOPE + C_ROPE)
    return pl.pallas_call(
        functools.partial(_mla_q_kernel, scale=scale),
        grid=(m // bm,),
        in_specs=[pl.BlockSpec((bm, C_Q_RANK), lambda i: (i, 0)),
                  pl.BlockSpec((1, C_Q_RANK), lambda i: (0, 0)),
                  pl.BlockSpec((None, C_HEADS * C_QW, C_Q_RANK), lambda i: (layer, 0, 0)),
                  pl.BlockSpec((LANES, bm), lambda i: (0, i % nt))],
        out_specs=pl.BlockSpec((C_HEADS * C_QW, bm), lambda i: (0, i)),
        out_shape=jax.ShapeDtypeStruct((C_HEADS * C_QW, m), BF16),
        compiler_params=_params("parallel"),
        name="mla_q_proj",
    )(c_all, q_norm.reshape(1, C_Q_RANK).astype(F32), wq_t, table_t)


def _mla_kv_kernel(c_ref, g_ref, wk_ref, wvt_ref, kr_ref, t_ref, ok_ref, ovt_ref, okr_ref):
    a = _rms_bf16(c_ref[...], g_ref[...])
    ok_ref[...] = jnp.dot(a, wk_ref[...], preferred_element_type=F32).astype(ok_ref.dtype)
    ovt_ref[0] = lax.dot_general(wvt_ref[...], a, _NT, preferred_element_type=F32).astype(ovt_ref.dtype)
    t = kr_ref[...] * t_ref[...]
    rope = t + pltpu.roll(t, C_ROPE, axis=1)
    lane = lax.broadcasted_iota(jnp.int32, rope.shape, 1)
    okr_ref[...] = jnp.where(lane < C_ROPE, rope, 0.0).astype(okr_ref.dtype)


def mla_kv_proj(c_all, kv_norm, wk, wv_t, layer, table, seq, bm):
    m = c_all.shape[0]
    nt = seq // bm
    n = wk.shape[2]
    ckv_blk = C_Q_RANK // C_KV_RANK
    kr_blk = (C_Q_RANK + C_KV_RANK) // LANES
    return pl.pallas_call(
        _mla_kv_kernel,
        grid=(m // bm,),
        in_specs=[pl.BlockSpec((bm, C_KV_RANK), lambda i: (i, ckv_blk)),
                  pl.BlockSpec((1, C_KV_RANK), lambda i: (0, 0)),
                  pl.BlockSpec((None, C_KV_RANK, n), lambda i: (layer, 0, 0)),
                  pl.BlockSpec((None, n, C_KV_RANK), lambda i: (layer, 0, 0)),
                  pl.BlockSpec((bm, LANES), lambda i: (i, kr_blk)),
                  pl.BlockSpec((bm, LANES), lambda i: (i % nt, 0))],
        out_specs=[pl.BlockSpec((bm, n), lambda i: (i, 0)),
                   pl.BlockSpec((1, n, bm), lambda i: (i, 0, 0)),
                   pl.BlockSpec((bm, LANES), lambda i: (i, 0))],
        out_shape=[jax.ShapeDtypeStruct((m, n), BF16),
                   jax.ShapeDtypeStruct((m // bm, n, bm), BF16),
                   jax.ShapeDtypeStruct((m, LANES), BF16)],
        compiler_params=_params("parallel"),
        name="mla_kv_proj",
    )(c_all, kv_norm.reshape(1, C_KV_RANK).astype(F32), wk, wv_t, c_all, table)


def _mla_attn_kernel(qt_ref, kn_ref, kr_ref, vt_ref, o_ref, s_ref, m_ref, acc_ref, *, bk, nkv, cw):
    bq = qt_ref.shape[1]
    cols = [slice(c * cw, (c + 1) * cw) for c in range(bq // cw)]
    m_ref[...] = jnp.full(m_ref.shape, NEG_INF, F32)
    acc_ref[...] = jnp.zeros(acc_ref.shape, F32)
    ones = jnp.ones((C_ONES_ROWS, bk), BF16)

    def keys(t):
        off = pl.multiple_of(t * bk, bk)
        return jnp.concatenate([kn_ref[pl.ds(off, bk), :], kr_ref[pl.ds(off, bk), :]], axis=1)

    ncb = len(cols)

    def scores(k, c):
        return jnp.dot(k, qt_ref[:, cols[c]], preferred_element_type=F32)

    def softmax_update(slot, cs):
        s = s_ref[slot]
        m_old = m_ref[:, cs]
        m_new = jnp.maximum(m_old, jnp.max(s, axis=0, keepdims=True))
        m_ref[:, cs] = m_new
        return jnp.exp2(s - m_new).astype(BF16), jnp.exp2(m_old - m_new)

    def step(t, last):
        vt = jnp.concatenate([vt_ref[t], ones], axis=0)
        k_cur = keys(t)
        k_nxt = None if last else keys(t + 1)
        for c, cs in enumerate(cols):
            p, alpha = softmax_update(c, cs)
            ahead = c + C_LOOKAHEAD
            if ahead < ncb:
                s_ref[ahead] = scores(k_cur, ahead)
            elif not last:
                s_ref[ahead - ncb] = scores(k_nxt, ahead - ncb)
            acc_ref[:, cs] = alpha * acc_ref[:, cs] + jnp.dot(vt, p, preferred_element_type=F32)

    k0 = keys(0)
    for c in range(C_LOOKAHEAD):
        s_ref[c] = scores(k0, c)

    def body(t, carry):
        step(t, False)
        return carry

    lax.fori_loop(0, nkv - 1, body, 0)
    step(nkv - 1, True)
    out_t = acc_ref[:C_V, :] / acc_ref[C_V:C_V + 1, :]
    o_ref[...] = out_t.T.astype(o_ref.dtype)


def mla_attention(q_t, kn, kr, v_t, batch, seq, bk, bq=1024, cw=256):
    bq = min(bq, seq)
    cw = min(cw, bq)
    nq = seq // bq
    nkv = seq // bk
    return pl.pallas_call(
        functools.partial(_mla_attn_kernel, bk=bk, nkv=nkv, cw=cw),
        grid=(batch, C_HEADS, nq),
        in_specs=[pl.BlockSpec((C_QW, bq), lambda b, h, i: (h, b * nq + i)),
                  pl.BlockSpec((seq, C_NOPE), lambda b, h, i: (b, h)),
                  pl.BlockSpec((seq, LANES), lambda b, h, i: (b, 0)),
                  pl.BlockSpec((nkv, C_V, bk), lambda b, h, i: (b, h, 0))],
        out_specs=pl.BlockSpec((bq, C_V), lambda b, h, i: (b * nq + i, h)),
        out_shape=jax.ShapeDtypeStruct((batch * seq, C_HEADS * C_V), BF16),
        scratch_shapes=[pltpu.VMEM((bq // cw, bk, cw), F32), pltpu.VMEM((1, bq), F32),
                        pltpu.VMEM((C_V + C_ONES_ROWS, bq), F32)],
        compiler_params=_params("parallel", "parallel", "arbitrary"),
        name="mla_attention",
    )(q_t, kn, kr, v_t)


def _xattn_kernel(x_ref, g_ref, wq_ref, k_ref, v_ref, wo_ref, g2_ref, xo_ref, ho_ref, o_scr):
    x = x_ref[...]
    h = _rms_bf16(x, g_ref[...])
    q = jnp.dot(h, wq_ref[...], preferred_element_type=F32).astype(BF16)
    scale = 1.0 / math.sqrt(HEAD_DIM)
    heads = [slice(hh * HEAD_DIM, (hh + 1) * HEAD_DIM) for hh in range(X_HEADS)]
    scores = [lax.dot_general(q[:, sl], k_ref[:, sl], (((1,), (1,)), ((), ())), preferred_element_type=F32)
              for sl in heads]
    probs, inv_denoms = [], []
    for s in scores:
        s = s * (scale * LOG2E)
        mx = jnp.max(s, axis=-1, keepdims=True)
        p = jnp.exp2(s - mx)
        probs.append(p.astype(BF16))
        inv_denoms.append(1.0 / jnp.sum(p, axis=-1, keepdims=True))
    for sl, p, inv in zip(heads, probs, inv_denoms):
        o = jnp.dot(p, v_ref[:, sl], preferred_element_type=F32) * inv
        o_scr[:, sl] = o.astype(o_scr.dtype)
    x_new = x + jnp.dot(o_scr[...], wo_ref[...], preferred_element_type=F32)
    xo_ref[...] = x_new
    ho_ref[...] = _rms_bf16(x_new, g2_ref[...])


def xattn_block(x, g, wq, kv_mem, wo, g_next, layer, seq, n_mem, bm=512):
    m, d = x.shape
    bm = _tile(min(m, seq), bm)
    per_b = seq // bm
    xd = X_HEADS * HEAD_DIM
    once = pl.Buffered(1)
    return pl.pallas_call(
        _xattn_kernel,
        grid=(m // bm,),
        in_specs=[pl.BlockSpec((bm, d), lambda i: (i, 0)),
                  pl.BlockSpec((1, d), lambda i: (0, 0)),
                  pl.BlockSpec((None, d, xd), lambda i: (layer, 0, 0), pipeline_mode=once),
                  pl.BlockSpec((n_mem, xd), lambda i: (i // per_b, 0)),
                  pl.BlockSpec((n_mem, xd), lambda i: (i // per_b, 1)),
                  pl.BlockSpec((None, xd, d), lambda i: (layer, 0, 0), pipeline_mode=once),
                  pl.BlockSpec((1, d), lambda i: (0, 0))],
        out_specs=[pl.BlockSpec((bm, d), lambda i: (i, 0)), pl.BlockSpec((bm, d), lambda i: (i, 0))],
        out_shape=[jax.ShapeDtypeStruct((m, d), F32), jax.ShapeDtypeStruct((m, d), BF16)],
        scratch_shapes=[pltpu.VMEM((bm, xd), BF16)],
        compiler_params=_params("parallel"),
        name="xattn_block",
    )(x, g.reshape(1, d).astype(F32), wq, kv_mem, kv_mem, wo, g_next.reshape(1, d).astype(F32))


def _rot_cols(w):
    half = w.shape[-1] // 2
    return jnp.concatenate([-w[..., half:], w[..., :half]], axis=-1)


def _rot_rows(w_t):
    half = w_t.shape[-2] // 2
    return jnp.concatenate([-w_t[..., half:, :], w_t[..., :half, :]], axis=-2)


A_IN_W = (A_Q_HEADS + 2 * A_KV_HEADS) * HEAD_DIM
B_IN_W = 3 * B_HEADS * HEAD_DIM


def _prep_weights(w_in, c_w_q_b, c_w_kv_b, w_out, x_w_q, x_w_k, x_w_v, x_w_o, w_gate, w_up, w_down, ff_pad):
    nl = w_in.shape[0]
    c_off = A_IN_W + B_IN_W
    w_in_t = jnp.swapaxes(w_in, 1, 2)
    w_ckr = w_in_t[:, c_off + C_Q_RANK + C_KV_RANK:]
    w_c_t = jnp.concatenate([w_in_t[:, c_off:], _rot_rows(w_ckr)], axis=1)
    w_c = cast_transposed(w_c_t, w_c_t.shape[1])
    w_in = cast_transposed(w_in_t, c_off)

    wq = c_w_q_b.reshape(nl, C_Q_RANK, C_HEADS, C_NOPE + C_ROPE)
    wq = jnp.concatenate([wq, _rot_cols(wq[..., C_NOPE:])], axis=-1)
    wq_t = wq.reshape(nl, C_Q_RANK, C_HEADS * C_QW).astype(BF16).transpose(0, 2, 1)

    wkv = c_w_kv_b.reshape(nl, C_KV_RANK, C_HEADS, C_NOPE + C_V).astype(BF16)
    wk = wkv[..., :C_NOPE].reshape(nl, C_KV_RANK, C_HEADS * C_NOPE)
    wv_t = wkv[..., C_NOPE:].reshape(nl, C_KV_RANK, C_HEADS * C_V).transpose(0, 2, 1)

    return dict(w_in=w_in, w_c=w_c, wq_t=wq_t, wk=wk, wv_t=wv_t, w_out=w_out.astype(BF16),
                wg=cast_pad_cols(w_gate, ff_pad), wu=cast_pad_cols(w_up, ff_pad), wd=cast_pad_rows(w_down, ff_pad),
                x_w_q=x_w_q.astype(BF16), w_xkv=jnp.concatenate([x_w_k, x_w_v], axis=2).astype(BF16),
                x_w_o=x_w_o.astype(BF16))


def _rope_table(seq):
    inv = 1.0 / (ROPE_BASE ** (jnp.arange(0, C_ROPE, 2, dtype=F32) / C_ROPE))
    ang = jnp.arange(seq, dtype=F32)[:, None] * inv[None, :]
    cos, sin = jnp.cos(ang), jnp.sin(ang)
    return jnp.concatenate([cos, cos, sin, sin], axis=1)


def kernel(x, mem, ln_mix, w_in, a_sink, b_rpb, c_q_norm, c_w_q_b, c_kv_norm, c_w_kv_b, w_out,
           ln_xattn, ln_mem, x_w_q, x_w_k, x_w_v, x_w_o, ln_ffn, w_gate, w_up, w_down, ln_final):
    batch, seq, d = x.shape
    n_mem = mem.shape[1]
    depth = w_in.shape[0]
    ff = w_gate.shape[2]
    ff_pad = -(-ff // 1024) * 1024
    table = _rope_table(seq)
    table_t = table.T
    kv_tile = min(C_KV_TILE, seq)
    xs = x.reshape(batch * seq, d)
    mem2 = mem.reshape(batch * n_mem, d)
    w = _prep_weights(w_in, c_w_q_b, c_w_kv_b, w_out, x_w_q, x_w_k, x_w_v, x_w_o, w_gate, w_up, w_down, ff_pad)
    b_bias = _mixer_b_bias(b_rpb, min(B_WIN_ROWS, seq // GRID_W))
    for l in range(depth):
        h, c_all = norm_proj(xs, ln_mix[l], w["w_c"], l)
        qkv_a = matmul(h, w["w_in"], l, BF16, n=A_IN_W, bn=1280, name="w_in_a")
        qkv_b = matmul(h, w["w_in"], l, BF16, n=B_IN_W, col0=A_IN_W, name="w_in_b")
        ya = mixer_a(qkv_a, a_sink[l], batch, seq)
        yb = mixer_b(qkv_b, b_bias[l], batch, seq)
        q_t = mla_q_proj(c_all, c_q_norm[l], w["wq_t"], l, table_t, seq)
        kn_c, v_t, kr_c = mla_kv_proj(c_all, c_kv_norm[l], w["wk"], w["wv_t"], l, table, seq, kv_tile)
        yc = mla_attention(q_t, kn_c, kr_c, v_t, batch, seq, kv_tile)
        xs = matmul([ya, yb, yc], w["w_out"], l, F32, res=xs, name="w_out")
        mem_n = rmsnorm(mem2, ln_mem[l], BF16)
        kv_mem = matmul(mem_n, w["w_xkv"], l, BF16, name="xattn_kv")
        xs, h = xattn_block(xs, ln_xattn[l], w["x_w_q"], kv_mem, w["x_w_o"], ln_ffn[l], l, seq, n_mem)
        act = gate_up(h, w["wg"], w["wu"], l)
        xs = matmul(act, w["wd"], l, F32, res=xs, bk=ff_pad // 4, name="ffn_down")
    out = rmsnorm(xs, ln_final, F32)
    return out.reshape(batch, seq, d)
```

```python
import functools
import math

import numpy as np
import jax
import jax.numpy as jnp
from jax import lax
from jax.experimental import pallas as pl
from jax.experimental.pallas import tpu as pltpu

F32 = jnp.float32
BF16 = jnp.bfloat16

HEAD_DIM = 128
GRID_W = 64
A_Q_HEADS = 12
A_KV_HEADS = 4
A_REP = A_Q_HEADS // A_KV_HEADS
A_WINDOW = 128
A_BLOCK = 128
B_HEADS = 8
B_WIN_ROWS = 8
B_WIN_COLS = 16
B_ROW_GROUP = 16
C_HEADS = 12
C_Q_RANK = 1024
C_KV_RANK = 512
C_NOPE = 128
C_ROPE = 64
C_V = 128
ROPE_BASE = 10000.0
X_HEADS = 4
NORM_EPS = 1e-6
NEG_INF = -1e30
LOG2E = 1.4426950408889634

VMEM_LIMIT_BYTES = 58 * 1024 * 1024
LANES = 128


def _params(*sem):
    return pltpu.CompilerParams(dimension_semantics=sem, vmem_limit_bytes=VMEM_LIMIT_BYTES)


def _tile(n, pref):
    if n <= pref:
        return n
    t = (pref // LANES) * LANES
    while t >= LANES:
        if n % t == 0:
            return t
        t -= LANES
    return n


def _rmsnorm_kernel(x_ref, g_ref, o_ref):
    x = x_ref[...].astype(F32)
    inv = lax.rsqrt(jnp.mean(x * x, axis=-1, keepdims=True) + NORM_EPS)
    o_ref[...] = (x * inv * g_ref[...]).astype(o_ref.dtype)


def rmsnorm(x, g, out_dtype, bm=256):
    m, d = x.shape
    bm = min(bm, m)
    return pl.pallas_call(
        _rmsnorm_kernel,
        grid=(m // bm,),
        in_specs=[pl.BlockSpec((bm, d), lambda i: (i, 0)), pl.BlockSpec((1, d), lambda i: (0, 0))],
        out_specs=pl.BlockSpec((bm, d), lambda i: (i, 0)),
        out_shape=jax.ShapeDtypeStruct((m, d), out_dtype),
        compiler_params=_params("parallel"),
        name="rmsnorm",
    )(x, g.reshape(1, d).astype(F32))


def _cast_pad_cols_kernel(x_ref, o_ref):
    n = x_ref.shape[-1]
    o_ref[:, :n] = x_ref[...].astype(o_ref.dtype)
    if o_ref.shape[1] > n:
        o_ref[:, n:] = jnp.zeros((o_ref.shape[0], o_ref.shape[1] - n), o_ref.dtype)


def cast_pad_cols(w, n_pad, br=256):
    nl, r, n = w.shape
    br = _tile(r, br)
    return pl.pallas_call(
        _cast_pad_cols_kernel,
        grid=(nl, r // br),
        in_specs=[pl.BlockSpec((None, br, n), lambda l, i: (l, i, 0))],
        out_specs=pl.BlockSpec((None, br, n_pad), lambda l, i: (l, i, 0)),
        out_shape=jax.ShapeDtypeStruct((nl, r, n_pad), BF16),
        compiler_params=_params("parallel", "parallel"),
        name="cast_pad_cols",
    )(w)


def _cast_pad_rows_kernel(x_ref, o_ref, *, n_src_blocks):
    i = pl.program_id(1)

    @pl.when(i < n_src_blocks)
    def _():
        o_ref[...] = x_ref[...].astype(o_ref.dtype)

    @pl.when(i >= n_src_blocks)
    def _():
        o_ref[...] = jnp.zeros(o_ref.shape, o_ref.dtype)


def cast_pad_rows(w, r_pad, br=256):
    nl, r, n = w.shape
    br = math.gcd(math.gcd(r, r_pad), br)
    nsrc = r // br
    return pl.pallas_call(
        functools.partial(_cast_pad_rows_kernel, n_src_blocks=nsrc),
        grid=(nl, r_pad // br),
        in_specs=[pl.BlockSpec((None, br, n), lambda l, i: (l, jnp.minimum(i, nsrc - 1), 0))],
        out_specs=pl.BlockSpec((None, br, n), lambda l, i: (l, i, 0)),
        out_shape=jax.ShapeDtypeStruct((nl, r_pad, n), BF16),
        compiler_params=_params("parallel", "parallel"),
        name="cast_pad_rows",
    )(w)


def _cast_transposed_kernel(x_ref, o_ref):
    o_ref[...] = x_ref[...].T.astype(o_ref.dtype)


def cast_transposed(w_t, n_rows, br=512):
    nl, _, k = w_t.shape
    br = _tile(n_rows, br)
    return pl.pallas_call(
        _cast_transposed_kernel,
        grid=(nl, n_rows // br),
        in_specs=[pl.BlockSpec((None, br, k), lambda l, j: (l, j, 0))],
        out_specs=pl.BlockSpec((None, k, br), lambda l, j: (l, 0, j)),
        out_shape=jax.ShapeDtypeStruct((nl, k, n_rows), BF16),
        compiler_params=_params("parallel", "parallel"),
        name="cast_transposed",
    )(w_t)


def _mm_kernel(*refs, n_a, nk, has_res):
    a_refs = refs[:n_a]
    w_ref = refs[n_a]
    pos = n_a + 1
    r_ref = refs[pos] if has_res else None
    pos += int(has_res)
    o_ref = refs[pos]

    if nk > 1:
        @pl.when(pl.program_id(2) == 0)
        def _():
            o_ref[...] = r_ref[...] if has_res else jnp.zeros(o_ref.shape, o_ref.dtype)

    a = a_refs[0][...] if n_a == 1 else jnp.concatenate([r[...] for r in a_refs], axis=1)
    part = jnp.dot(a, w_ref[...], preferred_element_type=F32)
    if nk > 1:
        o_ref[...] += part
    else:
        if has_res:
            part = r_ref[...] + part
        o_ref[...] = part.astype(o_ref.dtype)


def matmul(a_list, w, layer, out_dtype, *, n=None, col0=0, res=None, bm=1024, bn=1024, bk=None, name="matmul"):
    if not isinstance(a_list, (list, tuple)):
        a_list = [a_list]
    m = a_list[0].shape[0]
    kdim = sum(a.shape[1] for a in a_list)
    assert w.shape[1] == kdim
    n = w.shape[2] - col0 if n is None else n
    bm = _tile(m, bm)
    bn = _tile(math.gcd(n, col0) if col0 else n, bn)
    assert n % bn == 0 and col0 % bn == 0
    n_a = len(a_list)
    if bk is None or n_a > 1:
        nk = 1
    else:
        assert kdim % bk == 0
        nk = kdim // bk
    in_specs = []
    for a in a_list:
        kd = a.shape[1] if nk == 1 else bk
        in_specs.append(pl.BlockSpec((bm, kd), lambda i, j, k: (i, k)))
    cb0 = col0 // bn
    in_specs.append(pl.BlockSpec((None, kdim if nk == 1 else bk, bn), lambda i, j, k: (layer, k, cb0 + j)))
    args = list(a_list) + [w]
    if res is not None:
        in_specs.append(pl.BlockSpec((bm, bn), lambda i, j, k: (i, j)))
        args.append(res)
    assert nk == 1 or out_dtype == F32
    return pl.pallas_call(
        functools.partial(_mm_kernel, n_a=n_a, nk=nk, has_res=res is not None),
        grid=(m // bm, n // bn, nk),
        in_specs=in_specs,
        out_specs=pl.BlockSpec((bm, bn), lambda i, j, k: (i, j)),
        out_shape=jax.ShapeDtypeStruct((m, n), out_dtype),
        compiler_params=_params("parallel", "parallel", "arbitrary"),
        name=name,
    )(*args)


def _norm_proj_kernel(x_ref, g_ref, w_ref, h_ref, o_ref):
    x = x_ref[...]
    inv = lax.rsqrt(jnp.mean(x * x, axis=-1, keepdims=True) + NORM_EPS)
    h = (x * inv * g_ref[...]).astype(h_ref.dtype)
    h_ref[...] = h
    o_ref[...] = jnp.dot(h, w_ref[...], preferred_element_type=F32)


def norm_proj(x, g, w, layer, bm=512):
    m, d = x.shape
    n = w.shape[2]
    bm = _tile(m, bm)
    return pl.pallas_call(
        _norm_proj_kernel,
        grid=(m // bm,),
        in_specs=[pl.BlockSpec((bm, d), lambda i: (i, 0)),
                  pl.BlockSpec((1, d), lambda i: (0, 0)),
                  pl.BlockSpec((None, d, n), lambda i: (layer, 0, 0), pipeline_mode=pl.Buffered(1))],
        out_specs=[pl.BlockSpec((bm, d), lambda i: (i, 0)), pl.BlockSpec((bm, n), lambda i: (i, 0))],
        out_shape=[jax.ShapeDtypeStruct((m, d), BF16), jax.ShapeDtypeStruct((m, n), F32)],
        compiler_params=_params("parallel"),
        name="norm_proj",
    )(x, g.reshape(1, d).astype(F32), w)


def _gate_up_kernel(h_ref, wg_ref, wu_ref, o_ref):
    h = h_ref[...]
    g = jnp.dot(h, wg_ref[...], preferred_element_type=F32)
    u = jnp.dot(h, wu_ref[...], preferred_element_type=F32)
    o_ref[...] = (jax.nn.silu(g) * u).astype(o_ref.dtype)


def gate_up(h, wg, wu, layer, bm=1024, bn=512):
    m, d = h.shape
    n = wg.shape[2]
    bm = _tile(m, bm)
    bn = _tile(n, bn)
    return pl.pallas_call(
        _gate_up_kernel,
        grid=(m // bm, n // bn),
        in_specs=[
            pl.BlockSpec((bm, d), lambda i, j: (i, 0)),
            pl.BlockSpec((None, d, bn), lambda i, j: (layer, 0, j)),
            pl.BlockSpec((None, d, bn), lambda i, j: (layer, 0, j)),
        ],
        out_specs=pl.BlockSpec((bm, bn), lambda i, j: (i, j)),
        out_shape=jax.ShapeDtypeStruct((m, n), BF16),
        compiler_params=_params("parallel", "parallel"),
        name="gate_up",
    )(h, wg, wu)


def _mixer_a_kernel(slope_ref, sink_ref, q_ref, kp_ref, kc_ref, kn_ref, vp_ref, vc_ref, vn_ref, o_ref,
                    *, tq, seq):
    i = pl.program_id(1)
    g = pl.program_id(2)
    nsub = tq // A_BLOCK
    kcat = jnp.concatenate([kp_ref[...], kc_ref[...], kn_ref[...]], axis=0)
    vcat = jnp.concatenate([vp_ref[...], vc_ref[...], vn_ref[...]], axis=0)
    qi = lax.broadcasted_iota(jnp.int32, (A_BLOCK, 3 * A_BLOCK), 0)
    kj = lax.broadcasted_iota(jnp.int32, (A_BLOCK, 3 * A_BLOCK), 1)
    rel = kj - A_BLOCK - qi
    dist = jnp.abs(rel)
    dist_f = dist.astype(F32)
    in_window = dist <= A_WINDOW
    scale = 1.0 / math.sqrt(HEAD_DIM)
    units = [(j, r) for j in range(nsub) for r in range(A_REP)]
    scores = []
    for j, r in units:
        kband = kcat[j * A_BLOCK:(j + 3) * A_BLOCK]
        q = q_ref[j * A_BLOCK:(j + 1) * A_BLOCK, r * HEAD_DIM:(r + 1) * HEAD_DIM]
        scores.append(lax.dot_general(q, kband, (((1,), (1,)), ((), ())), preferred_element_type=F32))
    alibi = [jnp.where(in_window, -(slope_ref[g * A_REP + r] * LOG2E) * dist_f, NEG_INF) for r in range(A_REP)]
    in_seq = {}
    for j in {0, nsub - 1}:
        kpos = (i * nsub + j - 1) * A_BLOCK + kj
        in_seq[j] = (kpos >= 0) & (kpos < seq)
    probs, inv_denoms = [], []
    for (j, r), s in zip(units, scores):
        sink = sink_ref[g * A_REP + r] * LOG2E
        s = s * (scale * LOG2E) + alibi[r]
        if j in in_seq:
            s = jnp.where(in_seq[j], s, NEG_INF)
        mx = jnp.maximum(jnp.max(s, axis=-1, keepdims=True), sink)
        p = jnp.exp2(s - mx)
        denom = jnp.sum(p, axis=-1, keepdims=True) + jnp.exp2(sink - mx)
        probs.append(p.astype(BF16))
        inv_denoms.append(1.0 / denom)
    for (j, r), p, inv in zip(units, probs, inv_denoms):
        vband = vcat[j * A_BLOCK:(j + 3) * A_BLOCK]
        o = jnp.dot(p, vband, preferred_element_type=F32) * inv
        o_ref[j * A_BLOCK:(j + 1) * A_BLOCK, r * HEAD_DIM:(r + 1) * HEAD_DIM] = o.astype(o_ref.dtype)


def mixer_a(qkv, sink, batch, seq, tq=512):
    tq = min(tq, seq)
    nq = seq // tq
    nsub = tq // A_BLOCK
    nblk = seq // A_BLOCK
    kcol = A_Q_HEADS
    vcol = A_Q_HEADS + A_KV_HEADS
    slopes = 2.0 ** (-8.0 * jnp.arange(1, A_Q_HEADS + 1, dtype=F32) / A_Q_HEADS)

    def cur(col):
        return pl.BlockSpec((tq, HEAD_DIM), lambda b, i, g: (b * nq + i, col + g))

    def prev(col):
        return pl.BlockSpec((A_BLOCK, HEAD_DIM),
                            lambda b, i, g: (b * nblk + jnp.maximum(i * nsub - 1, 0), col + g))

    def nxt(col):
        return pl.BlockSpec((A_BLOCK, HEAD_DIM),
                            lambda b, i, g: (b * nblk + jnp.minimum((i + 1) * nsub, nblk - 1), col + g))

    smem = pl.BlockSpec(memory_space=pltpu.SMEM)
    return pl.pallas_call(
        functools.partial(_mixer_a_kernel, tq=tq, seq=seq),
        grid=(batch, nq, A_KV_HEADS),
        in_specs=[smem, smem,
                  pl.BlockSpec((tq, A_REP * HEAD_DIM), lambda b, i, g: (b * nq + i, g)),
                  prev(kcol), cur(kcol), nxt(kcol), prev(vcol), cur(vcol), nxt(vcol)],
        out_specs=pl.BlockSpec((tq, A_REP * HEAD_DIM), lambda b, i, g: (b * nq + i, g)),
        out_shape=jax.ShapeDtypeStruct((batch * seq, A_Q_HEADS * HEAD_DIM), BF16),
        compiler_params=_params("parallel", "parallel", "parallel"),
        name="mixer_a",
    )(slopes, sink.astype(F32), qkv, qkv, qkv, qkv, qkv, qkv, qkv)


def _mixer_b_kernel(q_ref, k_ref, v_ref, bias_ref, o_ref, *, rows, wr, group):
    scale = 1.0 / math.sqrt(HEAD_DIM)
    nkeys = wr * GRID_W

    def row_slice(r, n):
        return pl.ds(pl.multiple_of(r * GRID_W, GRID_W), n)

    def body(gi, carry):
        rws = [gi * group + j for j in range(group)]
        starts = [jnp.clip(r - wr // 2, 0, rows - wr) for r in rws]
        scores = [lax.dot_general(q_ref[row_slice(r, GRID_W), :], k_ref[row_slice(rs, nkeys), :],
                                  (((1,), (1,)), ((), ())), preferred_element_type=F32)
                  for r, rs in zip(rws, starts)]
        probs, inv_denoms = [], []
        for r, rs, s in zip(rws, starts, scores):
            s = s * (scale * LOG2E) + bias_ref[0, r - rs]
            mx = jnp.max(s, axis=-1, keepdims=True)
            p = jnp.exp2(s - mx)
            probs.append(p.astype(BF16))
            inv_denoms.append(1.0 / jnp.sum(p, axis=-1, keepdims=True))
        for r, rs, p, inv in zip(rws, starts, probs, inv_denoms):
            o = jnp.dot(p, v_ref[row_slice(rs, nkeys), :], preferred_element_type=F32) * inv
            o_ref[row_slice(r, GRID_W), :] = o.astype(o_ref.dtype)
        return carry

    lax.fori_loop(0, rows // group, body, 0)


def _mixer_b_bias(rpb, wr):
    qc = np.arange(GRID_W)[:, None]
    kc = np.arange(GRID_W)[None, :]
    cstart = np.clip(qc - B_WIN_COLS // 2, 0, GRID_W - B_WIN_COLS)
    col_ok = (kc >= cstart) & (kc < cstart + B_WIN_COLS)
    lo = GRID_W - B_WIN_COLS
    tbl = jnp.pad(rpb.astype(F32), ((0, 0), (0, 0), (0, 0), (lo, lo)), mode="edge")
    b = jnp.stack([tbl[..., GRID_W - 1 - q:2 * GRID_W - 1 - q] for q in range(GRID_W)], axis=-2)
    b = jnp.stack([b[:, :, B_WIN_ROWS - 1 - v:B_WIN_ROWS - 1 - v + wr] for v in range(wr)], axis=2)
    b = jnp.where(jnp.asarray(col_ok), b * LOG2E, NEG_INF)
    b = b.transpose(0, 1, 2, 4, 3, 5)
    return b.reshape(rpb.shape[0], rpb.shape[1], wr, GRID_W, wr * GRID_W)


def mixer_b(qkv, bias, batch, seq):
    rows = seq // GRID_W
    wr = min(B_WIN_ROWS, rows)
    group = math.gcd(rows, B_ROW_GROUP)

    def col(off):
        return pl.BlockSpec((seq, HEAD_DIM), lambda b, h: (b, off + h))

    return pl.pallas_call(
        functools.partial(_mixer_b_kernel, rows=rows, wr=wr, group=group),
        grid=(batch, B_HEADS),
        in_specs=[col(0), col(B_HEADS), col(2 * B_HEADS),
                  pl.BlockSpec((1, wr, GRID_W, wr * GRID_W), lambda b, h: (h, 0, 0, 0))],
        out_specs=pl.BlockSpec((seq, HEAD_DIM), lambda b, h: (b, h)),
        out_shape=jax.ShapeDtypeStruct((batch * seq, B_HEADS * HEAD_DIM), BF16),
        compiler_params=_params("parallel", "parallel"),
        name="mixer_b",
    )(qkv, qkv, qkv, bias)


C_QW = 2 * HEAD_DIM
C_ONES_ROWS = 16
C_KV_TILE = 2048
C_LOOKAHEAD = 2
_NT = (((1,), (1,)), ((), ()))


def _rms_bf16(c, g):
    inv = lax.rsqrt(jnp.mean(c * c, axis=-1, keepdims=True) + NORM_EPS)
    return (c * inv * g).astype(BF16)


def _mla_q_kernel(c_ref, g_ref, w_ref, t_ref, o_ref, *, scale):
    a = _rms_bf16(c_ref[...], g_ref[...])
    table = t_ref[...]
    for h in range(C_HEADS):
        r0 = h * C_QW
        acc = lax.dot_general(w_ref[r0:r0 + C_QW, :], a, _NT, preferred_element_type=F32)
        o_ref[r0:r0 + C_NOPE, :] = (acc[:C_NOPE] * scale).astype(o_ref.dtype)
        t = acc[C_NOPE:] * table
        rope = (t[:C_ROPE] + t[C_ROPE:]) * scale
        o_ref[r0 + C_NOPE:r0 + C_NOPE + C_ROPE, :] = rope.astype(o_ref.dtype)
        o_ref[r0 + C_NOPE + C_ROPE:r0 + C_QW, :] = jnp.zeros((C_ROPE, rope.shape[1]), o_ref.dtype)


def mla_q_proj(c_all, q_norm, wq_t, layer, table_t, seq, bm=512):
    m = c_all.shape[0]
    bm = _tile(min(m, seq), bm)
    nt = seq // bm
    scale = LOG2E / math.sqrt(C_NOPE + C_ROPE)
    return pl.pallas_call(
        functools.partial(_mla_q_kernel, scale=scale),
        grid=(m // bm,),
        in_specs=[pl.BlockSpec((bm, C_Q_RANK), lambda i: (i, 0)),
                  pl.BlockSpec((1, C_Q_RANK), lambda i: (0, 0)),
                  pl.BlockSpec((None, C_HEADS * C_QW, C_Q_RANK), lambda i: (layer, 0, 0)),
                  pl.BlockSpec((LANES, bm), lambda i: (0, i % nt))],
        out_specs=pl.BlockSpec((C_HEADS * C_QW, bm), lambda i: (0, i)),
        out_shape=jax.ShapeDtypeStruct((C_HEADS * C_QW, m), BF16),
        compiler_params=_params("parallel"),
        name="mla_q_proj",
    )(c_all, q_norm.reshape(1, C_Q_RANK).astype(F32), wq_t, table_t)


def _mla_kv_kernel(c_ref, g_ref, wk_ref, wvt_ref, kr_ref, t_ref, ok_ref, ovt_ref, okr_ref):
    a = _rms_bf16(c_ref[...], g_ref[...])
    ok_ref[...] = jnp.dot(a, wk_ref[...], preferred_element_type=F32).astype(ok_ref.dtype)
    ovt_ref[0] = lax.dot_general(wvt_ref[...], a, _NT, preferred_element_type=F32).astype(ovt_ref.dtype)
    t = kr_ref[...] * t_ref[...]
    rope = t + pltpu.roll(t, C_ROPE, axis=1)
    lane = lax.broadcasted_iota(jnp.int32, rope.shape, 1)
    okr_ref[...] = jnp.where(lane < C_ROPE, rope, 0.0).astype(okr_ref.dtype)


def mla_kv_proj(c_all, kv_norm, wk, wv_t, layer, table, seq, bm):
    m = c_all.shape[0]
    nt = seq // bm
    n = wk.shape[2]
    ckv_blk = C_Q_RANK // C_KV_RANK
    kr_blk = (C_Q_RANK + C_KV_RANK) // LANES
    return pl.pallas_call(
        _mla_kv_kernel,
        grid=(m // bm,),
        in_specs=[pl.BlockSpec((bm, C_KV_RANK), lambda i: (i, ckv_blk)),
                  pl.BlockSpec((1, C_KV_RANK), lambda i: (0, 0)),
                  pl.BlockSpec((None, C_KV_RANK, n), lambda i: (layer, 0, 0)),
                  pl.BlockSpec((None, n, C_KV_RANK), lambda i: (layer, 0, 0)),
                  pl.BlockSpec((bm, LANES), lambda i: (i, kr_blk)),
                  pl.BlockSpec((bm, LANES), lambda i: (i % nt, 0))],
        out_specs=[pl.BlockSpec((bm, n), lambda i: (i, 0)),
                   pl.BlockSpec((1, n, bm), lambda i: (i, 0, 0)),
                   pl.BlockSpec((bm, LANES), lambda i: (i, 0))],
        out_shape=[jax.ShapeDtypeStruct((m, n), BF16),
                   jax.ShapeDtypeStruct((m // bm, n, bm), BF16),
                   jax.ShapeDtypeStruct((m, LANES), BF16)],
        compiler_params=_params("parallel"),
        name="mla_kv_proj",
    )(c_all, kv_norm.reshape(1, C_KV_RANK).astype(F32), wk, wv_t, c_all, table)


def _mla_attn_kernel(qt_ref, kn_ref, kr_ref, vt_ref, o_ref, s_ref, m_ref, acc_ref, *, bk, nkv, cw):
    bq = qt_ref.shape[1]
    cols = [slice(c * cw, (c + 1) * cw) for c in range(bq // cw)]
    m_ref[...] = jnp.full(m_ref.shape, NEG_INF, F32)
    acc_ref[...] = jnp.zeros(acc_ref.shape, F32)
    ones = jnp.ones((C_ONES_ROWS, bk), BF16)

    def keys(t):
        off = pl.multiple_of(t * bk, bk)
        return jnp.concatenate([kn_ref[pl.ds(off, bk), :], kr_ref[pl.ds(off, bk), :]], axis=1)

    ncb = len(cols)

    def scores(k, c):
        return jnp.dot(k, qt_ref[:, cols[c]], preferred_element_type=F32)

    def softmax_update(slot, cs):
        s = s_ref[slot]
        m_old = m_ref[:, cs]
        m_new = jnp.maximum(m_old, jnp.max(s, axis=0, keepdims=True))
        m_ref[:, cs] = m_new
        return jnp.exp2(s - m_new).astype(BF16), jnp.exp2(m_old - m_new)

    def step(t, last):
        vt = jnp.concatenate([vt_ref[t], ones], axis=0)
        k_cur = keys(t)
        k_nxt = None if last else keys(t + 1)
        for c, cs in enumerate(cols):
            p, alpha = softmax_update(c, cs)
            ahead = c + C_LOOKAHEAD
            if ahead < ncb:
                s_ref[ahead] = scores(k_cur, ahead)
            elif not last:
                s_ref[ahead - ncb] = scores(k_nxt, ahead - ncb)
            acc_ref[:, cs] = alpha * acc_ref[:, cs] + jnp.dot(vt, p, preferred_element_type=F32)

    k0 = keys(0)
    for c in range(C_LOOKAHEAD):
        s_ref[c] = scores(k0, c)

    def body(t, carry):
        step(t, False)
        return carry

    lax.fori_loop(0, nkv - 1, body, 0)
    step(nkv - 1, True)
    out_t = acc_ref[:C_V, :] / acc_ref[C_V:C_V + 1, :]
    o_ref[...] = out_t.T.astype(o_ref.dtype)


def mla_attention(q_t, kn, kr, v_t, batch, seq, bk, bq=2048, cw=256):
    bq = min(bq, seq)
    cw = min(cw, bq)
    nq = seq // bq
    nkv = seq // bk
    return pl.pallas_call(
        functools.partial(_mla_attn_kernel, bk=bk, nkv=nkv, cw=cw),
        grid=(batch, C_HEADS, nq),
        in_specs=[pl.BlockSpec((C_QW, bq), lambda b, h, i: (h, b * nq + i)),
                  pl.BlockSpec((seq, C_NOPE), lambda b, h, i: (b, h)),
                  pl.BlockSpec((seq, LANES), lambda b, h, i: (b, 0)),
                  pl.BlockSpec((nkv, C_V, bk), lambda b, h, i: (b, h, 0))],
        out_specs=pl.BlockSpec((bq, C_V), lambda b, h, i: (b * nq + i, h)),
        out_shape=jax.ShapeDtypeStruct((batch * seq, C_HEADS * C_V), BF16),
        scratch_shapes=[pltpu.VMEM((bq // cw, bk, cw), F32), pltpu.VMEM((1, bq), F32),
                        pltpu.VMEM((C_V + C_ONES_ROWS, bq), F32)],
        compiler_params=_params("parallel", "parallel", "arbitrary"),
        name="mla_attention",
    )(q_t, kn, kr, v_t)


def _xattn_kernel(x_ref, g_ref, wq_ref, k_ref, v_ref, wo_ref, g2_ref, xo_ref, ho_ref, o_scr):
    x = x_ref[...]
    h = _rms_bf16(x, g_ref[...])
    q = jnp.dot(h, wq_ref[...], preferred_element_type=F32).astype(BF16)
    scale = 1.0 / math.sqrt(HEAD_DIM)
    heads = [slice(hh * HEAD_DIM, (hh + 1) * HEAD_DIM) for hh in range(X_HEADS)]
    scores = [lax.dot_general(q[:, sl], k_ref[:, sl], (((1,), (1,)), ((), ())), preferred_element_type=F32)
              for sl in heads]
    probs, inv_denoms = [], []
    for s in scores:
        s = s * (scale * LOG2E)
        mx = jnp.max(s, axis=-1, keepdims=True)
        p = jnp.exp2(s - mx)
        probs.append(p.astype(BF16))
        inv_denoms.append(1.0 / jnp.sum(p, axis=-1, keepdims=True))
    for sl, p, inv in zip(heads, probs, inv_denoms):
        o = jnp.dot(p, v_ref[:, sl], preferred_element_type=F32) * inv
        o_scr[:, sl] = o.astype(o_scr.dtype)
    x_new = x + jnp.dot(o_scr[...], wo_ref[...], preferred_element_type=F32)
    xo_ref[...] = x_new
    ho_ref[...] = _rms_bf16(x_new, g2_ref[...])


def xattn_block(x, g, wq, kv_mem, wo, g_next, layer, seq, n_mem, bm=512):
    m, d = x.shape
    bm = _tile(min(m, seq), bm)
    per_b = seq // bm
    xd = X_HEADS * HEAD_DIM
    once = pl.Buffered(1)
    return pl.pallas_call(
        _xattn_kernel,
        grid=(m // bm,),
        in_specs=[pl.BlockSpec((bm, d), lambda i: (i, 0)),
                  pl.BlockSpec((1, d), lambda i: (0, 0)),
                  pl.BlockSpec((None, d, xd), lambda i: (layer, 0, 0), pipeline_mode=once),
                  pl.BlockSpec((n_mem, xd), lambda i: (i // per_b, 0)),
                  pl.BlockSpec((n_mem, xd), lambda i: (i // per_b, 1)),
                  pl.BlockSpec((None, xd, d), lambda i: (layer, 0, 0), pipeline_mode=once),
                  pl.BlockSpec((1, d), lambda i: (0, 0))],
        out_specs=[pl.BlockSpec((bm, d), lambda i: (i, 0)), pl.BlockSpec((bm, d), lambda i: (i, 0))],
        out_shape=[jax.ShapeDtypeStruct((m, d), F32), jax.ShapeDtypeStruct((m, d), BF16)],
        scratch_shapes=[pltpu.VMEM((bm, xd), BF16)],
        compiler_params=_params("parallel"),
        name="xattn_block",
    )(x, g.reshape(1, d).astype(F32), wq, kv_mem, kv_mem, wo, g_next.reshape(1, d).astype(F32))


def _rot_cols(w):
    half = w.shape[-1] // 2
    return jnp.concatenate([-w[..., half:], w[..., :half]], axis=-1)


def _rot_rows(w_t):
    half = w_t.shape[-2] // 2
    return jnp.concatenate([-w_t[..., half:, :], w_t[..., :half, :]], axis=-2)


A_IN_W = (A_Q_HEADS + 2 * A_KV_HEADS) * HEAD_DIM
B_IN_W = 3 * B_HEADS * HEAD_DIM


def _prep_weights(w_in, c_w_q_b, c_w_kv_b, w_out, x_w_q, x_w_k, x_w_v, x_w_o, w_gate, w_up, w_down, ff_pad):
    nl = w_in.shape[0]
    c_off = A_IN_W + B_IN_W
    w_in_t = jnp.swapaxes(w_in, 1, 2)
    w_ckr = w_in_t[:, c_off + C_Q_RANK + C_KV_RANK:]
    w_c_t = jnp.concatenate([w_in_t[:, c_off:], _rot_rows(w_ckr)], axis=1)
    w_c = cast_transposed(w_c_t, w_c_t.shape[1])
    w_in = cast_transposed(w_in_t, c_off)

    wq = c_w_q_b.reshape(nl, C_Q_RANK, C_HEADS, C_NOPE + C_ROPE)
    wq = jnp.concatenate([wq, _rot_cols(wq[..., C_NOPE:])], axis=-1)
    wq_t = wq.reshape(nl, C_Q_RANK, C_HEADS * C_QW).astype(BF16).transpose(0, 2, 1)

    wkv = c_w_kv_b.reshape(nl, C_KV_RANK, C_HEADS, C_NOPE + C_V).astype(BF16)
    wk = wkv[..., :C_NOPE].reshape(nl, C_KV_RANK, C_HEADS * C_NOPE)
    wv_t = wkv[..., C_NOPE:].reshape(nl, C_KV_RANK, C_HEADS * C_V).transpose(0, 2, 1)

    return dict(w_in=w_in, w_c=w_c, wq_t=wq_t, wk=wk, wv_t=wv_t, w_out=w_out.astype(BF16),
                wg=cast_pad_cols(w_gate, ff_pad), wu=cast_pad_cols(w_up, ff_pad), wd=cast_pad_rows(w_down, ff_pad),
                x_w_q=x_w_q.astype(BF16), w_xkv=jnp.concatenate([x_w_k, x_w_v], axis=2).astype(BF16),
                x_w_o=x_w_o.astype(BF16))


def _rope_table(seq):
    inv = 1.0 / (ROPE_BASE ** (jnp.arange(0, C_ROPE, 2, dtype=F32) / C_ROPE))
    ang = jnp.arange(seq, dtype=F32)[:, None] * inv[None, :]
    cos, sin = jnp.cos(ang), jnp.sin(ang)
    return jnp.concatenate([cos, cos, sin, sin], axis=1)


def kernel(x, mem, ln_mix, w_in, a_sink, b_rpb, c_q_norm, c_w_q_b, c_kv_norm, c_w_kv_b, w_out,
           ln_xattn, ln_mem, x_w_q, x_w_k, x_w_v, x_w_o, ln_ffn, w_gate, w_up, w_down, ln_final):
    batch, seq, d = x.shape
    n_mem = mem.shape[1]
    depth = w_in.shape[0]
    ff = w_gate.shape[2]
    ff_pad = -(-ff // 1024) * 1024
    table = _rope_table(seq)
    table_t = table.T
    kv_tile = min(C_KV_TILE, seq)
    xs = x.reshape(batch * seq, d)
    mem2 = mem.reshape(batch * n_mem, d)
    w = _prep_weights(w_in, c_w_q_b, c_w_kv_b, w_out, x_w_q, x_w_k, x_w_v, x_w_o, w_gate, w_up, w_down, ff_pad)
    b_bias = _mixer_b_bias(b_rpb, min(B_WIN_ROWS, seq // GRID_W))
    for l in range(depth):
        h, c_all = norm_proj(xs, ln_mix[l], w["w_c"], l)
        qkv_a = matmul(h, w["w_in"], l, BF16, n=A_IN_W, bn=1280, name="w_in_a")
        qkv_b = matmul(h, w["w_in"], l, BF16, n=B_IN_W, col0=A_IN_W, name="w_in_b")
        ya = mixer_a(qkv_a, a_sink[l], batch, seq)
        yb = mixer_b(qkv_b, b_bias[l], batch, seq)
        q_t = mla_q_proj(c_all, c_q_norm[l], w["wq_t"], l, table_t, seq)
        kn_c, v_t, kr_c = mla_kv_proj(c_all, c_kv_norm[l], w["wk"], w["wv_t"], l, table, seq, kv_tile)
        yc = mla_attention(q_t, kn_c, kr_c, v_t, batch, seq, kv_tile)
        xs = matmul([ya, yb, yc], w["w_out"], l, F32, res=xs, name="w_out")
        mem_n = rmsnorm(mem2, ln_mem[l], BF16)
        kv_mem = matmul(mem_n, w["w_xkv"], l, BF16, name="xattn_kv")
        xs, h = xattn_block(xs, ln_xattn[l], w["x_w_q"], kv_mem, w["x_w_o"], ln_ffn[l], l, seq, n_mem)
        act = gate_up(h, w["wg"], w["wu"], l)
        xs = matmul(act, w["wd"], l, F32, res=xs, bk=ff_pad // 4, name="ffn_down")
    out = rmsnorm(xs, ln_final, F32)
    return out.reshape(batch, seq, d)
```

```python
import functools
import math

import numpy as np
import jax
import jax.numpy as jnp
from jax import lax
from jax.experimental import pallas as pl
from jax.experimental.pallas import tpu as pltpu

F32 = jnp.float32
BF16 = jnp.bfloat16

HEAD_DIM = 128
GRID_W = 64
A_Q_HEADS = 12
A_KV_HEADS = 4
A_REP = A_Q_HEADS // A_KV_HEADS
A_WINDOW = 128
A_BLOCK = 128
B_HEADS = 8
B_WIN_ROWS = 8
B_WIN_COLS = 16
B_ROW_GROUP = 64
C_HEADS = 12
C_Q_RANK = 1024
C_KV_RANK = 512
C_NOPE = 128
C_ROPE = 64
C_V = 128
ROPE_BASE = 10000.0
X_HEADS = 4
NORM_EPS = 1e-6
NEG_INF = -1e30
LOG2E = 1.4426950408889634

VMEM_LIMIT_BYTES = 58 * 1024 * 1024
LANES = 128


def _params(*sem):
    return pltpu.CompilerParams(dimension_semantics=sem, vmem_limit_bytes=VMEM_LIMIT_BYTES)


def _tile(n, pref):
    if n <= pref:
        return n
    t = (pref // LANES) * LANES
    while t >= LANES:
        if n % t == 0:
            return t
        t -= LANES
    return n


def _rmsnorm_kernel(x_ref, g_ref, o_ref):
    x = x_ref[...].astype(F32)
    inv = lax.rsqrt(jnp.mean(x * x, axis=-1, keepdims=True) + NORM_EPS)
    o_ref[...] = (x * inv * g_ref[...]).astype(o_ref.dtype)


def rmsnorm(x, g, out_dtype, bm=256):
    m, d = x.shape
    bm = min(bm, m)
    return pl.pallas_call(
        _rmsnorm_kernel,
        grid=(m // bm,),
        in_specs=[pl.BlockSpec((bm, d), lambda i: (i, 0)), pl.BlockSpec((1, d), lambda i: (0, 0))],
        out_specs=pl.BlockSpec((bm, d), lambda i: (i, 0)),
        out_shape=jax.ShapeDtypeStruct((m, d), out_dtype),
        compiler_params=_params("parallel"),
        name="rmsnorm",
    )(x, g.reshape(1, d).astype(F32))


def _cast_pad_cols_kernel(x_ref, o_ref):
    n = x_ref.shape[-1]
    o_ref[:, :n] = x_ref[...].astype(o_ref.dtype)
    if o_ref.shape[1] > n:
        o_ref[:, n:] = jnp.zeros((o_ref.shape[0], o_ref.shape[1] - n), o_ref.dtype)


def cast_pad_cols(w, n_pad, br=256):
    nl, r, n = w.shape
    br = _tile(r, br)
    return pl.pallas_call(
        _cast_pad_cols_kernel,
        grid=(nl, r // br),
        in_specs=[pl.BlockSpec((None, br, n), lambda l, i: (l, i, 0))],
        out_specs=pl.BlockSpec((None, br, n_pad), lambda l, i: (l, i, 0)),
        out_shape=jax.ShapeDtypeStruct((nl, r, n_pad), BF16),
        compiler_params=_params("parallel", "parallel"),
        name="cast_pad_cols",
    )(w)


def _cast_pad_rows_kernel(x_ref, o_ref, *, n_src_blocks):
    i = pl.program_id(1)

    @pl.when(i < n_src_blocks)
    def _():
        o_ref[...] = x_ref[...].astype(o_ref.dtype)

    @pl.when(i >= n_src_blocks)
    def _():
        o_ref[...] = jnp.zeros(o_ref.shape, o_ref.dtype)


def cast_pad_rows(w, r_pad, br=256):
    nl, r, n = w.shape
    br = math.gcd(math.gcd(r, r_pad), br)
    nsrc = r // br
    return pl.pallas_call(
        functools.partial(_cast_pad_rows_kernel, n_src_blocks=nsrc),
        grid=(nl, r_pad // br),
        in_specs=[pl.BlockSpec((None, br, n), lambda l, i: (l, jnp.minimum(i, nsrc - 1), 0))],
        out_specs=pl.BlockSpec((None, br, n), lambda l, i: (l, i, 0)),
        out_shape=jax.ShapeDtypeStruct((nl, r_pad, n), BF16),
        compiler_params=_params("parallel", "parallel"),
        name="cast_pad_rows",
    )(w)


def _cast_transposed_kernel(x_ref, o_ref):
    o_ref[...] = x_ref[...].T.astype(o_ref.dtype)


def cast_transposed(w_t, n_rows, br=512):
    nl, _, k = w_t.shape
    br = _tile(n_rows, br)
    return pl.pallas_call(
        _cast_transposed_kernel,
        grid=(nl, n_rows // br),
        in_specs=[pl.BlockSpec((None, br, k), lambda l, j: (l, j, 0))],
        out_specs=pl.BlockSpec((None, k, br), lambda l, j: (l, 0, j)),
        out_shape=jax.ShapeDtypeStruct((nl, k, n_rows), BF16),
        compiler_params=_params("parallel", "parallel"),
        name="cast_transposed",
    )(w_t)


def _mm_kernel(*refs, n_a, nk, has_res):
    a_refs = refs[:n_a]
    w_ref = refs[n_a]
    pos = n_a + 1
    r_ref = refs[pos] if has_res else None
    pos += int(has_res)
    o_ref = refs[pos]

    if nk > 1:
        @pl.when(pl.program_id(2) == 0)
        def _():
            o_ref[...] = r_ref[...] if has_res else jnp.zeros(o_ref.shape, o_ref.dtype)

    a = a_refs[0][...] if n_a == 1 else jnp.concatenate([r[...] for r in a_refs], axis=1)
    part = jnp.dot(a, w_ref[...], preferred_element_type=F32)
    if nk > 1:
        o_ref[...] += part
    else:
        if has_res:
            part = r_ref[...] + part
        o_ref[...] = part.astype(o_ref.dtype)


def matmul(a_list, w, layer, out_dtype, *, n=None, col0=0, res=None, bm=1024, bn=1024, bk=None, name="matmul"):
    if not isinstance(a_list, (list, tuple)):
        a_list = [a_list]
    m = a_list[0].shape[0]
    kdim = sum(a.shape[1] for a in a_list)
    assert w.shape[1] == kdim
    n = w.shape[2] - col0 if n is None else n
    bm = _tile(m, bm)
    bn = _tile(math.gcd(n, col0) if col0 else n, bn)
    assert n % bn == 0 and col0 % bn == 0
    n_a = len(a_list)
    if bk is None or n_a > 1:
        nk = 1
    else:
        assert kdim % bk == 0
        nk = kdim // bk
    in_specs = []
    for a in a_list:
        kd = a.shape[1] if nk == 1 else bk
        in_specs.append(pl.BlockSpec((bm, kd), lambda i, j, k: (i, k)))
    cb0 = col0 // bn
    in_specs.append(pl.BlockSpec((None, kdim if nk == 1 else bk, bn), lambda i, j, k: (layer, k, cb0 + j)))
    args = list(a_list) + [w]
    if res is not None:
        in_specs.append(pl.BlockSpec((bm, bn), lambda i, j, k: (i, j)))
        args.append(res)
    assert nk == 1 or out_dtype == F32
    return pl.pallas_call(
        functools.partial(_mm_kernel, n_a=n_a, nk=nk, has_res=res is not None),
        grid=(m // bm, n // bn, nk),
        in_specs=in_specs,
        out_specs=pl.BlockSpec((bm, bn), lambda i, j, k: (i, j)),
        out_shape=jax.ShapeDtypeStruct((m, n), out_dtype),
        compiler_params=_params("parallel", "parallel", "arbitrary"),
        name=name,
    )(*args)


def _norm_proj_kernel(x_ref, g_ref, w_ref, h_ref, o_ref):
    x = x_ref[...]
    inv = lax.rsqrt(jnp.mean(x * x, axis=-1, keepdims=True) + NORM_EPS)
    h = (x * inv * g_ref[...]).astype(h_ref.dtype)
    h_ref[...] = h
    o_ref[...] = jnp.dot(h, w_ref[...], preferred_element_type=F32)


def norm_proj(x, g, w, layer, bm=512):
    m, d = x.shape
    n = w.shape[2]
    bm = _tile(m, bm)
    return pl.pallas_call(
        _norm_proj_kernel,
        grid=(m // bm,),
        in_specs=[pl.BlockSpec((bm, d), lambda i: (i, 0)),
                  pl.BlockSpec((1, d), lambda i: (0, 0)),
                  pl.BlockSpec((None, d, n), lambda i: (layer, 0, 0), pipeline_mode=pl.Buffered(1))],
        out_specs=[pl.BlockSpec((bm, d), lambda i: (i, 0)), pl.BlockSpec((bm, n), lambda i: (i, 0))],
        out_shape=[jax.ShapeDtypeStruct((m, d), BF16), jax.ShapeDtypeStruct((m, n), F32)],
        compiler_params=_params("parallel"),
        name="norm_proj",
    )(x, g.reshape(1, d).astype(F32), w)


def _gate_up_kernel(h_ref, wg_ref, wu_ref, o_ref):
    h = h_ref[...]
    g = jnp.dot(h, wg_ref[...], preferred_element_type=F32)
    u = jnp.dot(h, wu_ref[...], preferred_element_type=F32)
    o_ref[...] = (jax.nn.silu(g) * u).astype(o_ref.dtype)


def gate_up(h, wg, wu, layer, bm=1024, bn=512):
    m, d = h.shape
    n = wg.shape[2]
    bm = _tile(m, bm)
    bn = _tile(n, bn)
    return pl.pallas_call(
        _gate_up_kernel,
        grid=(m // bm, n // bn),
        in_specs=[
            pl.BlockSpec((bm, d), lambda i, j: (i, 0)),
            pl.BlockSpec((None, d, bn), lambda i, j: (layer, 0, j)),
            pl.BlockSpec((None, d, bn), lambda i, j: (layer, 0, j)),
        ],
        out_specs=pl.BlockSpec((bm, bn), lambda i, j: (i, j)),
        out_shape=jax.ShapeDtypeStruct((m, n), BF16),
        compiler_params=_params("parallel", "parallel"),
        name="gate_up",
    )(h, wg, wu)


def _mixer_a_kernel(slope_ref, sink_ref, q_ref, kp_ref, kc_ref, kn_ref, vp_ref, vc_ref, vn_ref, o_ref,
                    *, tq, seq):
    i = pl.program_id(1)
    g = pl.program_id(2)
    nsub = tq // A_BLOCK
    kcat = jnp.concatenate([kp_ref[...], kc_ref[...], kn_ref[...]], axis=0)
    vcat = jnp.concatenate([vp_ref[...], vc_ref[...], vn_ref[...]], axis=0)
    qi = lax.broadcasted_iota(jnp.int32, (A_BLOCK, 3 * A_BLOCK), 0)
    kj = lax.broadcasted_iota(jnp.int32, (A_BLOCK, 3 * A_BLOCK), 1)
    rel = kj - A_BLOCK - qi
    dist = jnp.abs(rel)
    dist_f = dist.astype(F32)
    in_window = dist <= A_WINDOW
    scale = 1.0 / math.sqrt(HEAD_DIM)
    units = [(j, r) for j in range(nsub) for r in range(A_REP)]
    scores = []
    for j, r in units:
        kband = kcat[j * A_BLOCK:(j + 3) * A_BLOCK]
        q = q_ref[j * A_BLOCK:(j + 1) * A_BLOCK, r * HEAD_DIM:(r + 1) * HEAD_DIM]
        scores.append(lax.dot_general(q, kband, (((1,), (1,)), ((), ())), preferred_element_type=F32))
    alibi = [jnp.where(in_window, -(slope_ref[g * A_REP + r] * LOG2E) * dist_f, NEG_INF) for r in range(A_REP)]
    in_seq = {}
    for j in {0, nsub - 1}:
        kpos = (i * nsub + j - 1) * A_BLOCK + kj
        in_seq[j] = (kpos >= 0) & (kpos < seq)
    probs, inv_denoms = [], []
    for (j, r), s in zip(units, scores):
        sink = sink_ref[g * A_REP + r] * LOG2E
        s = s * (scale * LOG2E) + alibi[r]
        if j in in_seq:
            s = jnp.where(in_seq[j], s, NEG_INF)
        mx = jnp.maximum(jnp.max(s, axis=-1, keepdims=True), sink)
        p = jnp.exp2(s - mx)
        denom = jnp.sum(p, axis=-1, keepdims=True) + jnp.exp2(sink - mx)
        probs.append(p.astype(BF16))
        inv_denoms.append(1.0 / denom)
    for (j, r), p, inv in zip(units, probs, inv_denoms):
        vband = vcat[j * A_BLOCK:(j + 3) * A_BLOCK]
        o = jnp.dot(p, vband, preferred_element_type=F32) * inv
        o_ref[j * A_BLOCK:(j + 1) * A_BLOCK, r * HEAD_DIM:(r + 1) * HEAD_DIM] = o.astype(o_ref.dtype)


def mixer_a(qkv, sink, batch, seq, tq=2048):
    tq = min(tq, seq)
    nq = seq // tq
    nsub = tq // A_BLOCK
    nblk = seq // A_BLOCK
    kcol = A_Q_HEADS
    vcol = A_Q_HEADS + A_KV_HEADS
    slopes = 2.0 ** (-8.0 * jnp.arange(1, A_Q_HEADS + 1, dtype=F32) / A_Q_HEADS)

    def cur(col):
        return pl.BlockSpec((tq, HEAD_DIM), lambda b, i, g: (b * nq + i, col + g))

    def prev(col):
        return pl.BlockSpec((A_BLOCK, HEAD_DIM),
                            lambda b, i, g: (b * nblk + jnp.maximum(i * nsub - 1, 0), col + g))

    def nxt(col):
        return pl.BlockSpec((A_BLOCK, HEAD_DIM),
                            lambda b, i, g: (b * nblk + jnp.minimum((i + 1) * nsub, nblk - 1), col + g))

    smem = pl.BlockSpec(memory_space=pltpu.SMEM)
    return pl.pallas_call(
        functools.partial(_mixer_a_kernel, tq=tq, seq=seq),
        grid=(batch, nq, A_KV_HEADS),
        in_specs=[smem, smem,
                  pl.BlockSpec((tq, A_REP * HEAD_DIM), lambda b, i, g: (b * nq + i, g)),
                  prev(kcol), cur(kcol), nxt(kcol), prev(vcol), cur(vcol), nxt(vcol)],
        out_specs=pl.BlockSpec((tq, A_REP * HEAD_DIM), lambda b, i, g: (b * nq + i, g)),
        out_shape=jax.ShapeDtypeStruct((batch * seq, A_Q_HEADS * HEAD_DIM), BF16),
        compiler_params=_params("parallel", "parallel", "parallel"),
        name="mixer_a",
    )(slopes, sink.astype(F32), qkv, qkv, qkv, qkv, qkv, qkv, qkv)


def _mixer_b_kernel(q_ref, k_ref, v_ref, bias_ref, o_ref, *, rows, wr, group):
    scale = 1.0 / math.sqrt(HEAD_DIM)
    nkeys = wr * GRID_W

    def row_slice(r, n):
        return pl.ds(pl.multiple_of(r * GRID_W, GRID_W), n)

    def body(gi, carry):
        rws = [gi * group + j for j in range(group)]
        starts = [jnp.clip(r - wr // 2, 0, rows - wr) for r in rws]
        scores = [lax.dot_general(q_ref[row_slice(r, GRID_W), :], k_ref[row_slice(rs, nkeys), :],
                                  (((1,), (1,)), ((), ())), preferred_element_type=F32)
                  for r, rs in zip(rws, starts)]
        probs, inv_denoms = [], []
        for r, rs, s in zip(rws, starts, scores):
            s = s * (scale * LOG2E) + bias_ref[0, r - rs]
            mx = jnp.max(s, axis=-1, keepdims=True)
            p = jnp.exp2(s - mx)
            probs.append(p.astype(BF16))
            inv_denoms.append(1.0 / jnp.sum(p, axis=-1, keepdims=True))
        for r, rs, p, inv in zip(rws, starts, probs, inv_denoms):
            o = jnp.dot(p, v_ref[row_slice(rs, nkeys), :], preferred_element_type=F32) * inv
            o_ref[row_slice(r, GRID_W), :] = o.astype(o_ref.dtype)
        return carry

    lax.fori_loop(0, rows // group, body, 0)


def _mixer_b_bias(rpb, wr):
    qc = np.arange(GRID_W)[:, None]
    kc = np.arange(GRID_W)[None, :]
    cstart = np.clip(qc - B_WIN_COLS // 2, 0, GRID_W - B_WIN_COLS)
    col_ok = (kc >= cstart) & (kc < cstart + B_WIN_COLS)
    lo = GRID_W - B_WIN_COLS
    tbl = jnp.pad(rpb.astype(F32), ((0, 0), (0, 0), (0, 0), (lo, lo)), mode="edge")
    b = jnp.stack([tbl[..., GRID_W - 1 - q:2 * GRID_W - 1 - q] for q in range(GRID_W)], axis=-2)
    b = jnp.stack([b[:, :, B_WIN_ROWS - 1 - v:B_WIN_ROWS - 1 - v + wr] for v in range(wr)], axis=2)
    b = jnp.where(jnp.asarray(col_ok), b * LOG2E, NEG_INF)
    b = b.transpose(0, 1, 2, 4, 3, 5)
    return b.reshape(rpb.shape[0], rpb.shape[1], wr, GRID_W, wr * GRID_W)


def mixer_b(qkv, bias, batch, seq):
    rows = seq // GRID_W
    wr = min(B_WIN_ROWS, rows)
    group = math.gcd(rows, B_ROW_GROUP)

    def col(off):
        return pl.BlockSpec((seq, HEAD_DIM), lambda b, h: (b, off + h))

    return pl.pallas_call(
        functools.partial(_mixer_b_kernel, rows=rows, wr=wr, group=group),
        grid=(batch, B_HEADS),
        in_specs=[col(0), col(B_HEADS), col(2 * B_HEADS),
                  pl.BlockSpec((1, wr, GRID_W, wr * GRID_W), lambda b, h: (h, 0, 0, 0))],
        out_specs=pl.BlockSpec((seq, HEAD_DIM), lambda b, h: (b, h)),
        out_shape=jax.ShapeDtypeStruct((batch * seq, B_HEADS * HEAD_DIM), BF16),
        compiler_params=_params("parallel", "parallel"),
        name="mixer_b",
    )(qkv, qkv, qkv, bias)


C_QW = 2 * HEAD_DIM
C_ONES_ROWS = 16
C_KV_TILE = 2048
C_LOOKAHEAD = 2
_NT = (((1,), (1,)), ((), ()))


def _rms_bf16(c, g):
    inv = lax.rsqrt(jnp.mean(c * c, axis=-1, keepdims=True) + NORM_EPS)
    return (c * inv * g).astype(BF16)


def _mla_q_kernel(c_ref, g_ref, w_ref, t_ref, o_ref, *, scale):
    a = _rms_bf16(c_ref[...], g_ref[...])
    table = t_ref[...]
    for h in range(C_HEADS):
        r0 = h * C_QW
        acc = lax.dot_general(w_ref[r0:r0 + C_QW, :], a, _NT, preferred_element_type=F32)
        o_ref[r0:r0 + C_NOPE, :] = (acc[:C_NOPE] * scale).astype(o_ref.dtype)
        t = acc[C_NOPE:] * table
        rope = (t[:C_ROPE] + t[C_ROPE:]) * scale
        o_ref[r0 + C_NOPE:r0 + C_NOPE + C_ROPE, :] = rope.astype(o_ref.dtype)
        o_ref[r0 + C_NOPE + C_ROPE:r0 + C_QW, :] = jnp.zeros((C_ROPE, rope.shape[1]), o_ref.dtype)


def mla_q_proj(c_all, q_norm, wq_t, layer, table_t, seq, bm=512):
    m = c_all.shape[0]
    bm = _tile(min(m, seq), bm)
    nt = seq // bm
    scale = LOG2E / math.sqrt(C_NOPE + C_ROPE)
    return pl.pallas_call(
        functools.partial(_mla_q_kernel, scale=scale),
        grid=(m // bm,),
        in_specs=[pl.BlockSpec((bm, C_Q_RANK), lambda i: (i, 0)),
                  pl.BlockSpec((1, C_Q_RANK), lambda i: (0, 0)),
                  pl.BlockSpec((None, C_HEADS * C_QW, C_Q_RANK), lambda i: (layer, 0, 0)),
                  pl.BlockSpec((LANES, bm), lambda i: (0, i % nt))],
        out_specs=pl.BlockSpec((C_HEADS * C_QW, bm), lambda i: (0, i)),
        out_shape=jax.ShapeDtypeStruct((C_HEADS * C_QW, m), BF16),
        compiler_params=_params("parallel"),
        name="mla_q_proj",
    )(c_all, q_norm.reshape(1, C_Q_RANK).astype(F32), wq_t, table_t)


def _mla_kv_kernel(c_ref, g_ref, wk_ref, wvt_ref, kr_ref, t_ref, ok_ref, ovt_ref, okr_ref):
    a = _rms_bf16(c_ref[...], g_ref[...])
    ok_ref[...] = jnp.dot(a, wk_ref[...], preferred_element_type=F32).astype(ok_ref.dtype)
    ovt_ref[0] = lax.dot_general(wvt_ref[...], a, _NT, preferred_element_type=F32).astype(ovt_ref.dtype)
    t = kr_ref[...] * t_ref[...]
    rope = t + pltpu.roll(t, C_ROPE, axis=1)
    lane = lax.broadcasted_iota(jnp.int32, rope.shape, 1)
    okr_ref[...] = jnp.where(lane < C_ROPE, rope, 0.0).astype(okr_ref.dtype)


def mla_kv_proj(c_all, kv_norm, wk, wv_t, layer, table, seq, bm):
    m = c_all.shape[0]
    nt = seq // bm
    n = wk.shape[2]
    ckv_blk = C_Q_RANK // C_KV_RANK
    kr_blk = (C_Q_RANK + C_KV_RANK) // LANES
    return pl.pallas_call(
        _mla_kv_kernel,
        grid=(m // bm,),
        in_specs=[pl.BlockSpec((bm, C_KV_RANK), lambda i: (i, ckv_blk)),
                  pl.BlockSpec((1, C_KV_RANK), lambda i: (0, 0)),
                  pl.BlockSpec((None, C_KV_RANK, n), lambda i: (layer, 0, 0)),
                  pl.BlockSpec((None, n, C_KV_RANK), lambda i: (layer, 0, 0)),
                  pl.BlockSpec((bm, LANES), lambda i: (i, kr_blk)),
                  pl.BlockSpec((bm, LANES), lambda i: (i % nt, 0))],
        out_specs=[pl.BlockSpec((bm, n), lambda i: (i, 0)),
                   pl.BlockSpec((1, n, bm), lambda i: (i, 0, 0)),
                   pl.BlockSpec((bm, LANES), lambda i: (i, 0))],
        out_shape=[jax.ShapeDtypeStruct((m, n), BF16),
                   jax.ShapeDtypeStruct((m // bm, n, bm), BF16),
                   jax.ShapeDtypeStruct((m, LANES), BF16)],
        compiler_params=_params("parallel"),
        name="mla_kv_proj",
    )(c_all, kv_norm.reshape(1, C_KV_RANK).astype(F32), wk, wv_t, c_all, table)


def _mla_attn_kernel(qt_ref, kn_ref, kr_ref, vt_ref, o_ref, s_ref, m_ref, acc_ref, *, bk, nkv, cw):
    bq = qt_ref.shape[1]
    cols = [slice(c * cw, (c + 1) * cw) for c in range(bq // cw)]
    m_ref[...] = jnp.full(m_ref.shape, NEG_INF, F32)
    acc_ref[...] = jnp.zeros(acc_ref.shape, F32)
    ones = jnp.ones((C_ONES_ROWS, bk), BF16)

    def keys(t):
        off = pl.multiple_of(t * bk, bk)
        return jnp.concatenate([kn_ref[pl.ds(off, bk), :], kr_ref[pl.ds(off, bk), :]], axis=1)

    ncb = len(cols)

    def scores(k, c):
        return jnp.dot(k, qt_ref[:, cols[c]], preferred_element_type=F32)

    def softmax_update(slot, cs):
        s = s_ref[slot]
        m_old = m_ref[:, cs]
        m_new = jnp.maximum(m_old, jnp.max(s, axis=0, keepdims=True))
        m_ref[:, cs] = m_new
        return jnp.exp2(s - m_new).astype(BF16), jnp.exp2(m_old - m_new)

    def step(t, last):
        vt = jnp.concatenate([vt_ref[t], ones], axis=0)
        k_cur = keys(t)
        k_nxt = None if last else keys(t + 1)
        for c, cs in enumerate(cols):
            p, alpha = softmax_update(c, cs)
            ahead = c + C_LOOKAHEAD
            if ahead < ncb:
                s_ref[ahead] = scores(k_cur, ahead)
            elif not last:
                s_ref[ahead - ncb] = scores(k_nxt, ahead - ncb)
            acc_ref[:, cs] = alpha * acc_ref[:, cs] + jnp.dot(vt, p, preferred_element_type=F32)

    k0 = keys(0)
    for c in range(C_LOOKAHEAD):
        s_ref[c] = scores(k0, c)

    def body(t, carry):
        step(t, False)
        return carry

    lax.fori_loop(0, nkv - 1, body, 0)
    step(nkv - 1, True)
    out_t = acc_ref[:C_V, :] / acc_ref[C_V:C_V + 1, :]
    o_ref[...] = out_t.T.astype(o_ref.dtype)


def mla_attention(q_t, kn, kr, v_t, batch, seq, bk, bq=2048, cw=256):
    bq = min(bq, seq)
    cw = min(cw, bq)
    nq = seq // bq
    nkv = seq // bk
    return pl.pallas_call(
        functools.partial(_mla_attn_kernel, bk=bk, nkv=nkv, cw=cw),
        grid=(batch, C_HEADS, nq),
        in_specs=[pl.BlockSpec((C_QW, bq), lambda b, h, i: (h, b * nq + i)),
                  pl.BlockSpec((seq, C_NOPE), lambda b, h, i: (b, h)),
                  pl.BlockSpec((seq, LANES), lambda b, h, i: (b, 0)),
                  pl.BlockSpec((nkv, C_V, bk), lambda b, h, i: (b, h, 0))],
        out_specs=pl.BlockSpec((bq, C_V), lambda b, h, i: (b * nq + i, h)),
        out_shape=jax.ShapeDtypeStruct((batch * seq, C_HEADS * C_V), BF16),
        scratch_shapes=[pltpu.VMEM((bq // cw, bk, cw), F32), pltpu.VMEM((1, bq), F32),
                        pltpu.VMEM((C_V + C_ONES_ROWS, bq), F32)],
        compiler_params=_params("parallel", "parallel", "arbitrary"),
        name="mla_attention",
    )(q_t, kn, kr, v_t)


def _xattn_kernel(x_ref, g_ref, wq_ref, k_ref, v_ref, wo_ref, g2_ref, xo_ref, ho_ref, o_scr):
    x = x_ref[...]
    h = _rms_bf16(x, g_ref[...])
    q = jnp.dot(h, wq_ref[...], preferred_element_type=F32).astype(BF16)
    scale = 1.0 / math.sqrt(HEAD_DIM)
    heads = [slice(hh * HEAD_DIM, (hh + 1) * HEAD_DIM) for hh in range(X_HEADS)]
    scores = [lax.dot_general(q[:, sl], k_ref[:, sl], (((1,), (1,)), ((), ())), preferred_element_type=F32)
              for sl in heads]
    probs, inv_denoms = [], []
    for s in scores:
        s = s * (scale * LOG2E)
        mx = jnp.max(s, axis=-1, keepdims=True)
        p = jnp.exp2(s - mx)
        probs.append(p.astype(BF16))
        inv_denoms.append(1.0 / jnp.sum(p, axis=-1, keepdims=True))
    for sl, p, inv in zip(heads, probs, inv_denoms):
        o = jnp.dot(p, v_ref[:, sl], preferred_element_type=F32) * inv
        o_scr[:, sl] = o.astype(o_scr.dtype)
    x_new = x + jnp.dot(o_scr[...], wo_ref[...], preferred_element_type=F32)
    xo_ref[...] = x_new
    ho_ref[...] = _rms_bf16(x_new, g2_ref[...])


def xattn_block(x, g, wq, kv_mem, wo, g_next, layer, seq, n_mem, bm=512):
    m, d = x.shape
    bm = _tile(min(m, seq), bm)
    per_b = seq // bm
    xd = X_HEADS * HEAD_DIM
    once = pl.Buffered(1)
    return pl.pallas_call(
        _xattn_kernel,
        grid=(m // bm,),
        in_specs=[pl.BlockSpec((bm, d), lambda i: (i, 0)),
                  pl.BlockSpec((1, d), lambda i: (0, 0)),
                  pl.BlockSpec((None, d, xd), lambda i: (layer, 0, 0), pipeline_mode=once),
                  pl.BlockSpec((n_mem, xd), lambda i: (i // per_b, 0)),
                  pl.BlockSpec((n_mem, xd), lambda i: (i // per_b, 1)),
                  pl.BlockSpec((None, xd, d), lambda i: (layer, 0, 0), pipeline_mode=once),
                  pl.BlockSpec((1, d), lambda i: (0, 0))],
        out_specs=[pl.BlockSpec((bm, d), lambda i: (i, 0)), pl.BlockSpec((bm, d), lambda i: (i, 0))],
        out_shape=[jax.ShapeDtypeStruct((m, d), F32), jax.ShapeDtypeStruct((m, d), BF16)],
        scratch_shapes=[pltpu.VMEM((bm, xd), BF16)],
        compiler_params=_params("parallel"),
        name="xattn_block",
    )(x, g.reshape(1, d).astype(F32), wq, kv_mem, kv_mem, wo, g_next.reshape(1, d).astype(F32))


def _rot_cols(w):
    half = w.shape[-1] // 2
    return jnp.concatenate([-w[..., half:], w[..., :half]], axis=-1)


def _rot_rows(w_t):
    half = w_t.shape[-2] // 2
    return jnp.concatenate([-w_t[..., half:, :], w_t[..., :half, :]], axis=-2)


A_IN_W = (A_Q_HEADS + 2 * A_KV_HEADS) * HEAD_DIM
B_IN_W = 3 * B_HEADS * HEAD_DIM


def _prep_weights(w_in, c_w_q_b, c_w_kv_b, w_out, x_w_q, x_w_k, x_w_v, x_w_o, w_gate, w_up, w_down, ff_pad):
    nl = w_in.shape[0]
    c_off = A_IN_W + B_IN_W
    w_in_t = jnp.swapaxes(w_in, 1, 2)
    w_ckr = w_in_t[:, c_off + C_Q_RANK + C_KV_RANK:]
    w_c_t = jnp.concatenate([w_in_t[:, c_off:], _rot_rows(w_ckr)], axis=1)
    w_c = cast_transposed(w_c_t, w_c_t.shape[1])
    w_in = cast_transposed(w_in_t, c_off)

    wq = c_w_q_b.reshape(nl, C_Q_RANK, C_HEADS, C_NOPE + C_ROPE)
    wq = jnp.concatenate([wq, _rot_cols(wq[..., C_NOPE:])], axis=-1)
    wq_t = wq.reshape(nl, C_Q_RANK, C_HEADS * C_QW).astype(BF16).transpose(0, 2, 1)

    wkv = c_w_kv_b.reshape(nl, C_KV_RANK, C_HEADS, C_NOPE + C_V).astype(BF16)
    wk = wkv[..., :C_NOPE].reshape(nl, C_KV_RANK, C_HEADS * C_NOPE)
    wv_t = wkv[..., C_NOPE:].reshape(nl, C_KV_RANK, C_HEADS * C_V).transpose(0, 2, 1)

    return dict(w_in=w_in, w_c=w_c, wq_t=wq_t, wk=wk, wv_t=wv_t, w_out=w_out.astype(BF16),
                wg=cast_pad_cols(w_gate, ff_pad), wu=cast_pad_cols(w_up, ff_pad), wd=cast_pad_rows(w_down, ff_pad),
                x_w_q=x_w_q.astype(BF16), w_xkv=jnp.concatenate([x_w_k, x_w_v], axis=2).astype(BF16),
                x_w_o=x_w_o.astype(BF16))


def _rope_table(seq):
    inv = 1.0 / (ROPE_BASE ** (jnp.arange(0, C_ROPE, 2, dtype=F32) / C_ROPE))
    ang = jnp.arange(seq, dtype=F32)[:, None] * inv[None, :]
    cos, sin = jnp.cos(ang), jnp.sin(ang)
    return jnp.concatenate([cos, cos, sin, sin], axis=1)


def kernel(x, mem, ln_mix, w_in, a_sink, b_rpb, c_q_norm, c_w_q_b, c_kv_norm, c_w_kv_b, w_out,
           ln_xattn, ln_mem, x_w_q, x_w_k, x_w_v, x_w_o, ln_ffn, w_gate, w_up, w_down, ln_final):
    batch, seq, d = x.shape
    n_mem = mem.shape[1]
    depth = w_in.shape[0]
    ff = w_gate.shape[2]
    ff_pad = -(-ff // 1024) * 1024
    table = _rope_table(seq)
    table_t = table.T
    kv_tile = min(C_KV_TILE, seq)
    xs = x.reshape(batch * seq, d)
    mem2 = mem.reshape(batch * n_mem, d)
    w = _prep_weights(w_in, c_w_q_b, c_w_kv_b, w_out, x_w_q, x_w_k, x_w_v, x_w_o, w_gate, w_up, w_down, ff_pad)
    b_bias = _mixer_b_bias(b_rpb, min(B_WIN_ROWS, seq // GRID_W))
    for l in range(depth):
        h, c_all = norm_proj(xs, ln_mix[l], w["w_c"], l)
        qkv_a = matmul(h, w["w_in"], l, BF16, n=A_IN_W, bn=1280, name="w_in_a")
        qkv_b = matmul(h, w["w_in"], l, BF16, n=B_IN_W, col0=A_IN_W, name="w_in_b")
        ya = mixer_a(qkv_a, a_sink[l], batch, seq)
        yb = mixer_b(qkv_b, b_bias[l], batch, seq)
        q_t = mla_q_proj(c_all, c_q_norm[l], w["wq_t"], l, table_t, seq)
        kn_c, v_t, kr_c = mla_kv_proj(c_all, c_kv_norm[l], w["wk"], w["wv_t"], l, table, seq, kv_tile)
        yc = mla_attention(q_t, kn_c, kr_c, v_t, batch, seq, kv_tile)
        xs = matmul([ya, yb, yc], w["w_out"], l, F32, res=xs, name="w_out")
        mem_n = rmsnorm(mem2, ln_mem[l], BF16)
        kv_mem = matmul(mem_n, w["w_xkv"], l, BF16, name="xattn_kv")
        xs, h = xattn_block(xs, ln_xattn[l], w["x_w_q"], kv_mem, w["x_w_o"], ln_ffn[l], l, seq, n_mem)
        act = gate_up(h, w["wg"], w["wu"], l)
        xs = matmul(act, w["wd"], l, F32, res=xs, bk=ff_pad // 4, name="ffn_down")
    out = rmsnorm(xs, ln_final, F32)
    return out.reshape(batch, seq, d)
```

```python
import functools
import math

import numpy as np
import jax
import jax.numpy as jnp
from jax import lax
from jax.experimental import pallas as pl
from jax.experimental.pallas import tpu as pltpu

F32 = jnp.float32
BF16 = jnp.bfloat16

HEAD_DIM = 128
GRID_W = 64
A_Q_HEADS = 12
A_KV_HEADS = 4
A_REP = A_Q_HEADS // A_KV_HEADS
A_WINDOW = 128
A_BLOCK = 128
B_HEADS = 8
B_WIN_ROWS = 8
B_WIN_COLS = 16
B_ROW_GROUP = 64
C_HEADS = 12
C_Q_RANK = 1024
C_KV_RANK = 512
C_NOPE = 128
C_ROPE = 64
C_V = 128
ROPE_BASE = 10000.0
X_HEADS = 4
NORM_EPS = 1e-6
NEG_INF = -1e30
LOG2E = 1.4426950408889634

VMEM_LIMIT_BYTES = 58 * 1024 * 1024
LANES = 128


def _params(*sem):
    return pltpu.CompilerParams(dimension_semantics=sem, vmem_limit_bytes=VMEM_LIMIT_BYTES)


def _tile(n, pref):
    if n <= pref:
        return n
    t = (pref // LANES) * LANES
    while t >= LANES:
        if n % t == 0:
            return t
        t -= LANES
    return n


def _rmsnorm_kernel(x_ref, g_ref, o_ref):
    x = x_ref[...].astype(F32)
    inv = lax.rsqrt(jnp.mean(x * x, axis=-1, keepdims=True) + NORM_EPS)
    o_ref[...] = (x * inv * g_ref[...]).astype(o_ref.dtype)


def rmsnorm(x, g, out_dtype, bm=256):
    m, d = x.shape
    bm = min(bm, m)
    return pl.pallas_call(
        _rmsnorm_kernel,
        grid=(m // bm,),
        in_specs=[pl.BlockSpec((bm, d), lambda i: (i, 0)), pl.BlockSpec((1, d), lambda i: (0, 0))],
        out_specs=pl.BlockSpec((bm, d), lambda i: (i, 0)),
        out_shape=jax.ShapeDtypeStruct((m, d), out_dtype),
        compiler_params=_params("parallel"),
        name="rmsnorm",
    )(x, g.reshape(1, d).astype(F32))


def _cast_pad_cols_kernel(x_ref, o_ref):
    n = x_ref.shape[-1]
    o_ref[:, :n] = x_ref[...].astype(o_ref.dtype)
    if o_ref.shape[1] > n:
        o_ref[:, n:] = jnp.zeros((o_ref.shape[0], o_ref.shape[1] - n), o_ref.dtype)


def cast_pad_cols(w, n_pad, br=256):
    nl, r, n = w.shape
    br = _tile(r, br)
    return pl.pallas_call(
        _cast_pad_cols_kernel,
        grid=(nl, r // br),
        in_specs=[pl.BlockSpec((None, br, n), lambda l, i: (l, i, 0))],
        out_specs=pl.BlockSpec((None, br, n_pad), lambda l, i: (l, i, 0)),
        out_shape=jax.ShapeDtypeStruct((nl, r, n_pad), BF16),
        compiler_params=_params("parallel", "parallel"),
        name="cast_pad_cols",
    )(w)


def _cast_pad_rows_kernel(x_ref, o_ref, *, n_src_blocks):
    i = pl.program_id(1)

    @pl.when(i < n_src_blocks)
    def _():
        o_ref[...] = x_ref[...].astype(o_ref.dtype)

    @pl.when(i >= n_src_blocks)
    def _():
        o_ref[...] = jnp.zeros(o_ref.shape, o_ref.dtype)


def cast_pad_rows(w, r_pad, br=256):
    nl, r, n = w.shape
    br = math.gcd(math.gcd(r, r_pad), br)
    nsrc = r // br
    return pl.pallas_call(
        functools.partial(_cast_pad_rows_kernel, n_src_blocks=nsrc),
        grid=(nl, r_pad // br),
        in_specs=[pl.BlockSpec((None, br, n), lambda l, i: (l, jnp.minimum(i, nsrc - 1), 0))],
        out_specs=pl.BlockSpec((None, br, n), lambda l, i: (l, i, 0)),
        out_shape=jax.ShapeDtypeStruct((nl, r_pad, n), BF16),
        compiler_params=_params("parallel", "parallel"),
        name="cast_pad_rows",
    )(w)


def _cast_transposed_kernel(x_ref, o_ref):
    o_ref[...] = x_ref[...].T.astype(o_ref.dtype)


def cast_transposed(w_t, n_rows, row0=0, br=512):
    nl, _, k = w_t.shape
    br = _tile(math.gcd(n_rows, row0) if row0 else n_rows, br)
    rb0 = row0 // br
    return pl.pallas_call(
        _cast_transposed_kernel,
        grid=(nl, n_rows // br),
        in_specs=[pl.BlockSpec((None, br, k), lambda l, j: (l, rb0 + j, 0))],
        out_specs=pl.BlockSpec((None, k, br), lambda l, j: (l, 0, j)),
        out_shape=jax.ShapeDtypeStruct((nl, k, n_rows), BF16),
        compiler_params=_params("parallel", "parallel"),
        name="cast_transposed",
    )(w_t)


def _mm_kernel(*refs, n_a, nk, has_res):
    a_refs = refs[:n_a]
    w_ref = refs[n_a]
    pos = n_a + 1
    r_ref = refs[pos] if has_res else None
    pos += int(has_res)
    o_ref = refs[pos]

    if nk > 1:
        @pl.when(pl.program_id(2) == 0)
        def _():
            o_ref[...] = r_ref[...] if has_res else jnp.zeros(o_ref.shape, o_ref.dtype)

    a = a_refs[0][...] if n_a == 1 else jnp.concatenate([r[...] for r in a_refs], axis=1)
    part = jnp.dot(a, w_ref[...], preferred_element_type=F32)
    if nk > 1:
        o_ref[...] += part
    else:
        if has_res:
            part = r_ref[...] + part
        o_ref[...] = part.astype(o_ref.dtype)


def matmul(a_list, w, layer, out_dtype, *, n=None, col0=0, res=None, bm=1024, bn=1024, bk=None, name="matmul"):
    if not isinstance(a_list, (list, tuple)):
        a_list = [a_list]
    m = a_list[0].shape[0]
    kdim = sum(a.shape[1] for a in a_list)
    assert w.shape[1] == kdim
    n = w.shape[2] - col0 if n is None else n
    bm = _tile(m, bm)
    bn = _tile(math.gcd(n, col0) if col0 else n, bn)
    assert n % bn == 0 and col0 % bn == 0
    n_a = len(a_list)
    if bk is None or n_a > 1:
        nk = 1
    else:
        assert kdim % bk == 0
        nk = kdim // bk
    in_specs = []
    for a in a_list:
        kd = a.shape[1] if nk == 1 else bk
        in_specs.append(pl.BlockSpec((bm, kd), lambda i, j, k: (i, k)))
    cb0 = col0 // bn
    in_specs.append(pl.BlockSpec((None, kdim if nk == 1 else bk, bn), lambda i, j, k: (layer, k, cb0 + j)))
    args = list(a_list) + [w]
    if res is not None:
        in_specs.append(pl.BlockSpec((bm, bn), lambda i, j, k: (i, j)))
        args.append(res)
    assert nk == 1 or out_dtype == F32
    return pl.pallas_call(
        functools.partial(_mm_kernel, n_a=n_a, nk=nk, has_res=res is not None),
        grid=(m // bm, n // bn, nk),
        in_specs=in_specs,
        out_specs=pl.BlockSpec((bm, bn), lambda i, j, k: (i, j)),
        out_shape=jax.ShapeDtypeStruct((m, n), out_dtype),
        compiler_params=_params("parallel", "parallel", "arbitrary"),
        name=name,
    )(*args)


def _norm_proj_kernel(x_ref, g_ref, w_ref, h_ref, o_ref):
    x = x_ref[...]
    inv = lax.rsqrt(jnp.mean(x * x, axis=-1, keepdims=True) + NORM_EPS)
    h = (x * inv * g_ref[...]).astype(h_ref.dtype)
    h_ref[...] = h
    o_ref[...] = jnp.dot(h, w_ref[...], preferred_element_type=F32)


def norm_proj(x, g, w, layer, bm=512):
    m, d = x.shape
    n = w.shape[2]
    bm = _tile(m, bm)
    return pl.pallas_call(
        _norm_proj_kernel,
        grid=(m // bm,),
        in_specs=[pl.BlockSpec((bm, d), lambda i: (i, 0)),
                  pl.BlockSpec((1, d), lambda i: (0, 0)),
                  pl.BlockSpec((None, d, n), lambda i: (layer, 0, 0), pipeline_mode=pl.Buffered(1))],
        out_specs=[pl.BlockSpec((bm, d), lambda i: (i, 0)), pl.BlockSpec((bm, n), lambda i: (i, 0))],
        out_shape=[jax.ShapeDtypeStruct((m, d), BF16), jax.ShapeDtypeStruct((m, n), F32)],
        compiler_params=_params("parallel"),
        name="norm_proj",
    )(x, g.reshape(1, d).astype(F32), w)


def _gate_up_kernel(h_ref, wg_ref, wu_ref, o_ref):
    h = h_ref[...]
    g = jnp.dot(h, wg_ref[...], preferred_element_type=F32)
    u = jnp.dot(h, wu_ref[...], preferred_element_type=F32)
    o_ref[...] = (jax.nn.silu(g) * u).astype(o_ref.dtype)


def gate_up(h, wg, wu, layer, bm=1024, bn=512):
    m, d = h.shape
    n = wg.shape[2]
    bm = _tile(m, bm)
    bn = _tile(n, bn)
    return pl.pallas_call(
        _gate_up_kernel,
        grid=(m // bm, n // bn),
        in_specs=[
            pl.BlockSpec((bm, d), lambda i, j: (i, 0)),
            pl.BlockSpec((None, d, bn), lambda i, j: (layer, 0, j)),
            pl.BlockSpec((None, d, bn), lambda i, j: (layer, 0, j)),
        ],
        out_specs=pl.BlockSpec((bm, bn), lambda i, j: (i, j)),
        out_shape=jax.ShapeDtypeStruct((m, n), BF16),
        compiler_params=_params("parallel", "parallel"),
        name="gate_up",
    )(h, wg, wu)


def _mixer_a_kernel(slope_ref, sink_ref, q_ref, kp_ref, kc_ref, kn_ref, vp_ref, vc_ref, vn_ref, o_ref,
                    *, tq, seq):
    i = pl.program_id(1)
    g = pl.program_id(2)
    nsub = tq // A_BLOCK
    kcat = jnp.concatenate([kp_ref[...], kc_ref[...], kn_ref[...]], axis=0)
    vcat = jnp.concatenate([vp_ref[...], vc_ref[...], vn_ref[...]], axis=0)
    qi = lax.broadcasted_iota(jnp.int32, (A_BLOCK, 3 * A_BLOCK), 0)
    kj = lax.broadcasted_iota(jnp.int32, (A_BLOCK, 3 * A_BLOCK), 1)
    rel = kj - A_BLOCK - qi
    dist = jnp.abs(rel)
    dist_f = dist.astype(F32)
    in_window = dist <= A_WINDOW
    scale = 1.0 / math.sqrt(HEAD_DIM)
    units = [(j, r) for j in range(nsub) for r in range(A_REP)]
    scores = []
    for j, r in units:
        kband = kcat[j * A_BLOCK:(j + 3) * A_BLOCK]
        q = q_ref[j * A_BLOCK:(j + 1) * A_BLOCK, r * HEAD_DIM:(r + 1) * HEAD_DIM]
        scores.append(lax.dot_general(q, kband, (((1,), (1,)), ((), ())), preferred_element_type=F32))
    alibi = [jnp.where(in_window, -(slope_ref[g * A_REP + r] * LOG2E) * dist_f, NEG_INF) for r in range(A_REP)]
    in_seq = {}
    for j in {0, nsub - 1}:
        kpos = (i * nsub + j - 1) * A_BLOCK + kj
        in_seq[j] = (kpos >= 0) & (kpos < seq)
    probs, inv_denoms = [], []
    for (j, r), s in zip(units, scores):
        sink = sink_ref[g * A_REP + r] * LOG2E
        s = s * (scale * LOG2E) + alibi[r]
        if j in in_seq:
            s = jnp.where(in_seq[j], s, NEG_INF)
        mx = jnp.maximum(jnp.max(s, axis=-1, keepdims=True), sink)
        p = jnp.exp2(s - mx)
        denom = jnp.sum(p, axis=-1, keepdims=True) + jnp.exp2(sink - mx)
        probs.append(p.astype(BF16))
        inv_denoms.append(1.0 / denom)
    for (j, r), p, inv in zip(units, probs, inv_denoms):
        vband = vcat[j * A_BLOCK:(j + 3) * A_BLOCK]
        o = jnp.dot(p, vband, preferred_element_type=F32) * inv
        o_ref[j * A_BLOCK:(j + 1) * A_BLOCK, r * HEAD_DIM:(r + 1) * HEAD_DIM] = o.astype(o_ref.dtype)


def mixer_a(qkv, sink, batch, seq, tq=2048):
    tq = min(tq, seq)
    nq = seq // tq
    nsub = tq // A_BLOCK
    nblk = seq // A_BLOCK
    kcol = A_Q_HEADS
    vcol = A_Q_HEADS + A_KV_HEADS
    slopes = 2.0 ** (-8.0 * jnp.arange(1, A_Q_HEADS + 1, dtype=F32) / A_Q_HEADS)

    def cur(col):
        return pl.BlockSpec((tq, HEAD_DIM), lambda b, i, g: (b * nq + i, col + g))

    def prev(col):
        return pl.BlockSpec((A_BLOCK, HEAD_DIM),
                            lambda b, i, g: (b * nblk + jnp.maximum(i * nsub - 1, 0), col + g))

    def nxt(col):
        return pl.BlockSpec((A_BLOCK, HEAD_DIM),
                            lambda b, i, g: (b * nblk + jnp.minimum((i + 1) * nsub, nblk - 1), col + g))

    smem = pl.BlockSpec(memory_space=pltpu.SMEM)
    return pl.pallas_call(
        functools.partial(_mixer_a_kernel, tq=tq, seq=seq),
        grid=(batch, nq, A_KV_HEADS),
        in_specs=[smem, smem,
                  pl.BlockSpec((tq, A_REP * HEAD_DIM), lambda b, i, g: (b * nq + i, g)),
                  prev(kcol), cur(kcol), nxt(kcol), prev(vcol), cur(vcol), nxt(vcol)],
        out_specs=pl.BlockSpec((tq, A_REP * HEAD_DIM), lambda b, i, g: (b * nq + i, g)),
        out_shape=jax.ShapeDtypeStruct((batch * seq, A_Q_HEADS * HEAD_DIM), BF16),
        compiler_params=_params("parallel", "parallel", "parallel"),
        name="mixer_a",
    )(slopes, sink.astype(F32), qkv, qkv, qkv, qkv, qkv, qkv, qkv)


def _mixer_b_kernel(q_ref, k_ref, v_ref, bias_ref, o_ref, *, rows, wr, group):
    scale = 1.0 / math.sqrt(HEAD_DIM)
    nkeys = wr * GRID_W

    def row_slice(r, n):
        return pl.ds(pl.multiple_of(r * GRID_W, GRID_W), n)

    def body(gi, carry):
        rws = [gi * group + j for j in range(group)]
        starts = [jnp.clip(r - wr // 2, 0, rows - wr) for r in rws]
        scores = [lax.dot_general(q_ref[row_slice(r, GRID_W), :], k_ref[row_slice(rs, nkeys), :],
                                  (((1,), (1,)), ((), ())), preferred_element_type=F32)
                  for r, rs in zip(rws, starts)]
        probs, inv_denoms = [], []
        for r, rs, s in zip(rws, starts, scores):
            s = s * (scale * LOG2E) + bias_ref[0, r - rs]
            mx = jnp.max(s, axis=-1, keepdims=True)
            p = jnp.exp2(s - mx)
            probs.append(p.astype(BF16))
            inv_denoms.append(1.0 / jnp.sum(p, axis=-1, keepdims=True))
        for r, rs, p, inv in zip(rws, starts, probs, inv_denoms):
            o = jnp.dot(p, v_ref[row_slice(rs, nkeys), :], preferred_element_type=F32) * inv
            o_ref[row_slice(r, GRID_W), :] = o.astype(o_ref.dtype)
        return carry

    lax.fori_loop(0, rows // group, body, 0)


def _mixer_b_bias(rpb, wr):
    qc = np.arange(GRID_W)[:, None]
    kc = np.arange(GRID_W)[None, :]
    cstart = np.clip(qc - B_WIN_COLS // 2, 0, GRID_W - B_WIN_COLS)
    col_ok = (kc >= cstart) & (kc < cstart + B_WIN_COLS)
    lo = GRID_W - B_WIN_COLS
    tbl = jnp.pad(rpb.astype(F32), ((0, 0), (0, 0), (0, 0), (lo, lo)), mode="edge")
    b = jnp.stack([tbl[..., GRID_W - 1 - q:2 * GRID_W - 1 - q] for q in range(GRID_W)], axis=-2)
    b = jnp.stack([b[:, :, B_WIN_ROWS - 1 - v:B_WIN_ROWS - 1 - v + wr] for v in range(wr)], axis=2)
    b = jnp.where(jnp.asarray(col_ok), b * LOG2E, NEG_INF)
    b = b.transpose(0, 1, 2, 4, 3, 5)
    return b.reshape(rpb.shape[0], rpb.shape[1], wr, GRID_W, wr * GRID_W)


def mixer_b(qkv, bias, batch, seq):
    rows = seq // GRID_W
    wr = min(B_WIN_ROWS, rows)
    group = math.gcd(rows, B_ROW_GROUP)

    def col(off):
        return pl.BlockSpec((seq, HEAD_DIM), lambda b, h: (b, off + h))

    return pl.pallas_call(
        functools.partial(_mixer_b_kernel, rows=rows, wr=wr, group=group),
        grid=(batch, B_HEADS),
        in_specs=[col(0), col(B_HEADS), col(2 * B_HEADS),
                  pl.BlockSpec((1, wr, GRID_W, wr * GRID_W), lambda b, h: (h, 0, 0, 0))],
        out_specs=pl.BlockSpec((seq, HEAD_DIM), lambda b, h: (b, h)),
        out_shape=jax.ShapeDtypeStruct((batch * seq, B_HEADS * HEAD_DIM), BF16),
        compiler_params=_params("parallel", "parallel"),
        name="mixer_b",
    )(qkv, qkv, qkv, bias)


C_QW = 2 * HEAD_DIM
C_ONES_ROWS = 16
C_KV_TILE = 2048
C_LOOKAHEAD = 2
_NT = (((1,), (1,)), ((), ()))


def _rms_bf16(c, g):
    inv = lax.rsqrt(jnp.mean(c * c, axis=-1, keepdims=True) + NORM_EPS)
    return (c * inv * g).astype(BF16)


def _mla_q_kernel(c_ref, g_ref, w_ref, t_ref, o_ref, *, scale):
    a = _rms_bf16(c_ref[...], g_ref[...])
    table = t_ref[...]
    for h in range(C_HEADS):
        r0 = h * C_QW
        acc = lax.dot_general(w_ref[r0:r0 + C_QW, :], a, _NT, preferred_element_type=F32)
        o_ref[r0:r0 + C_NOPE, :] = (acc[:C_NOPE] * scale).astype(o_ref.dtype)
        t = acc[C_NOPE:] * table
        rope = (t[:C_ROPE] + t[C_ROPE:]) * scale
        o_ref[r0 + C_NOPE:r0 + C_NOPE + C_ROPE, :] = rope.astype(o_ref.dtype)
        o_ref[r0 + C_NOPE + C_ROPE:r0 + C_QW, :] = jnp.zeros((C_ROPE, rope.shape[1]), o_ref.dtype)


def mla_q_proj(c_all, q_norm, wq_t, layer, table_t, seq, bm=512):
    m = c_all.shape[0]
    bm = _tile(min(m, seq), bm)
    nt = seq // bm
    scale = LOG2E / math.sqrt(C_NOPE + C_ROPE)
    return pl.pallas_call(
        functools.partial(_mla_q_kernel, scale=scale),
        grid=(m // bm,),
        in_specs=[pl.BlockSpec((bm, C_Q_RANK), lambda i: (i, 0)),
                  pl.BlockSpec((1, C_Q_RANK), lambda i: (0, 0)),
                  pl.BlockSpec((None, C_HEADS * C_QW, C_Q_RANK), lambda i: (layer, 0, 0)),
                  pl.BlockSpec((LANES, bm), lambda i: (0, i % nt))],
        out_specs=pl.BlockSpec((C_HEADS * C_QW, bm), lambda i: (0, i)),
        out_shape=jax.ShapeDtypeStruct((C_HEADS * C_QW, m), BF16),
        compiler_params=_params("parallel"),
        name="mla_q_proj",
    )(c_all, q_norm.reshape(1, C_Q_RANK).astype(F32), wq_t, table_t)


def _mla_kv_kernel(c_ref, g_ref, wk_ref, wvt_ref, kr_ref, t_ref, ok_ref, ovt_ref, okr_ref):
    a = _rms_bf16(c_ref[...], g_ref[...])
    ok_ref[...] = jnp.dot(a, wk_ref[...], preferred_element_type=F32).astype(ok_ref.dtype)
    ovt_ref[0] = lax.dot_general(wvt_ref[...], a, _NT, preferred_element_type=F32).astype(ovt_ref.dtype)
    t = kr_ref[...] * t_ref[...]
    rope = t + pltpu.roll(t, C_ROPE, axis=1)
    lane = lax.broadcasted_iota(jnp.int32, rope.shape, 1)
    okr_ref[...] = jnp.where(lane < C_ROPE, rope, 0.0).astype(okr_ref.dtype)


def mla_kv_proj(c_all, kv_norm, wk, wv_t, layer, table, seq, bm):
    m = c_all.shape[0]
    nt = seq // bm
    n = wk.shape[2]
    ckv_blk = C_Q_RANK // C_KV_RANK
    kr_blk = (C_Q_RANK + C_KV_RANK) // LANES
    return pl.pallas_call(
        _mla_kv_kernel,
        grid=(m // bm,),
        in_specs=[pl.BlockSpec((bm, C_KV_RANK), lambda i: (i, ckv_blk)),
                  pl.BlockSpec((1, C_KV_RANK), lambda i: (0, 0)),
                  pl.BlockSpec((None, C_KV_RANK, n), lambda i: (layer, 0, 0)),
                  pl.BlockSpec((None, n, C_KV_RANK), lambda i: (layer, 0, 0)),
                  pl.BlockSpec((bm, LANES), lambda i: (i, kr_blk)),
                  pl.BlockSpec((bm, LANES), lambda i: (i % nt, 0))],
        out_specs=[pl.BlockSpec((bm, n), lambda i: (i, 0)),
                   pl.BlockSpec((1, n, bm), lambda i: (i, 0, 0)),
                   pl.BlockSpec((bm, LANES), lambda i: (i, 0))],
        out_shape=[jax.ShapeDtypeStruct((m, n), BF16),
                   jax.ShapeDtypeStruct((m // bm, n, bm), BF16),
                   jax.ShapeDtypeStruct((m, LANES), BF16)],
        compiler_params=_params("parallel"),
        name="mla_kv_proj",
    )(c_all, kv_norm.reshape(1, C_KV_RANK).astype(F32), wk, wv_t, c_all, table)


def _mla_attn_kernel(qt_ref, kn_ref, kr_ref, vt_ref, o_ref, s_ref, m_ref, acc_ref, *, bk, nkv, cw):
    bq = qt_ref.shape[1]
    cols = [slice(c * cw, (c + 1) * cw) for c in range(bq // cw)]
    m_ref[...] = jnp.full(m_ref.shape, NEG_INF, F32)
    acc_ref[...] = jnp.zeros(acc_ref.shape, F32)
    ones = jnp.ones((C_ONES_ROWS, bk), BF16)

    def keys(t):
        off = pl.multiple_of(t * bk, bk)
        return jnp.concatenate([kn_ref[pl.ds(off, bk), :], kr_ref[pl.ds(off, bk), :]], axis=1)

    ncb = len(cols)

    def scores(k, c):
        return jnp.dot(k, qt_ref[:, cols[c]], preferred_element_type=F32)

    def softmax_update(slot, cs):
        s = s_ref[slot]
        m_old = m_ref[:, cs]
        m_new = jnp.maximum(m_old, jnp.max(s, axis=0, keepdims=True))
        m_ref[:, cs] = m_new
        return jnp.exp2(s - m_new).astype(BF16), jnp.exp2(m_old - m_new)

    def step(t, last):
        vt = jnp.concatenate([vt_ref[t], ones], axis=0)
        k_cur = keys(t)
        k_nxt = None if last else keys(t + 1)
        for c, cs in enumerate(cols):
            p, alpha = softmax_update(c, cs)
            ahead = c + C_LOOKAHEAD
            if ahead < ncb:
                s_ref[ahead] = scores(k_cur, ahead)
            elif not last:
                s_ref[ahead - ncb] = scores(k_nxt, ahead - ncb)
            acc_ref[:, cs] = alpha * acc_ref[:, cs] + jnp.dot(vt, p, preferred_element_type=F32)

    k0 = keys(0)
    for c in range(C_LOOKAHEAD):
        s_ref[c] = scores(k0, c)

    def body(t, carry):
        step(t, False)
        return carry

    lax.fori_loop(0, nkv - 1, body, 0)
    step(nkv - 1, True)
    out_t = acc_ref[:C_V, :] / acc_ref[C_V:C_V + 1, :]
    o_ref[...] = out_t.T.astype(o_ref.dtype)


def mla_attention(q_t, kn, kr, v_t, batch, seq, bk, bq=2048, cw=256):
    bq = min(bq, seq)
    cw = min(cw, bq)
    nq = seq // bq
    nkv = seq // bk
    return pl.pallas_call(
        functools.partial(_mla_attn_kernel, bk=bk, nkv=nkv, cw=cw),
        grid=(batch, C_HEADS, nq),
        in_specs=[pl.BlockSpec((C_QW, bq), lambda b, h, i: (h, b * nq + i)),
                  pl.BlockSpec((seq, C_NOPE), lambda b, h, i: (b, h)),
                  pl.BlockSpec((seq, LANES), lambda b, h, i: (b, 0)),
                  pl.BlockSpec((nkv, C_V, bk), lambda b, h, i: (b, h, 0))],
        out_specs=pl.BlockSpec((bq, C_V), lambda b, h, i: (b * nq + i, h)),
        out_shape=jax.ShapeDtypeStruct((batch * seq, C_HEADS * C_V), BF16),
        scratch_shapes=[pltpu.VMEM((bq // cw, bk, cw), F32), pltpu.VMEM((1, bq), F32),
                        pltpu.VMEM((C_V + C_ONES_ROWS, bq), F32)],
        compiler_params=_params("parallel", "parallel", "arbitrary"),
        name="mla_attention",
    )(q_t, kn, kr, v_t)


def _xattn_kernel(x_ref, g_ref, wq_ref, k_ref, v_ref, wo_ref, g2_ref, xo_ref, ho_ref, o_scr):
    x = x_ref[...]
    h = _rms_bf16(x, g_ref[...])
    q = jnp.dot(h, wq_ref[...], preferred_element_type=F32).astype(BF16)
    scale = 1.0 / math.sqrt(HEAD_DIM)
    heads = [slice(hh * HEAD_DIM, (hh + 1) * HEAD_DIM) for hh in range(X_HEADS)]
    scores = [lax.dot_general(q[:, sl], k_ref[:, sl], (((1,), (1,)), ((), ())), preferred_element_type=F32)
              for sl in heads]
    probs, inv_denoms = [], []
    for s in scores:
        s = s * (scale * LOG2E)
        mx = jnp.max(s, axis=-1, keepdims=True)
        p = jnp.exp2(s - mx)
        probs.append(p.astype(BF16))
        inv_denoms.append(1.0 / jnp.sum(p, axis=-1, keepdims=True))
    for sl, p, inv in zip(heads, probs, inv_denoms):
        o = jnp.dot(p, v_ref[:, sl], preferred_element_type=F32) * inv
        o_scr[:, sl] = o.astype(o_scr.dtype)
    x_new = x + jnp.dot(o_scr[...], wo_ref[...], preferred_element_type=F32)
    xo_ref[...] = x_new
    ho_ref[...] = _rms_bf16(x_new, g2_ref[...])


def xattn_block(x, g, wq, kv_mem, wo, g_next, layer, seq, n_mem, bm=512):
    m, d = x.shape
    bm = _tile(min(m, seq), bm)
    per_b = seq // bm
    xd = X_HEADS * HEAD_DIM
    once = pl.Buffered(1)
    return pl.pallas_call(
        _xattn_kernel,
        grid=(m // bm,),
        in_specs=[pl.BlockSpec((bm, d), lambda i: (i, 0)),
                  pl.BlockSpec((1, d), lambda i: (0, 0)),
                  pl.BlockSpec((None, d, xd), lambda i: (layer, 0, 0), pipeline_mode=once),
                  pl.BlockSpec((n_mem, xd), lambda i: (i // per_b, 0)),
                  pl.BlockSpec((n_mem, xd), lambda i: (i // per_b, 1)),
                  pl.BlockSpec((None, xd, d), lambda i: (layer, 0, 0), pipeline_mode=once),
                  pl.BlockSpec((1, d), lambda i: (0, 0))],
        out_specs=[pl.BlockSpec((bm, d), lambda i: (i, 0)), pl.BlockSpec((bm, d), lambda i: (i, 0))],
        out_shape=[jax.ShapeDtypeStruct((m, d), F32), jax.ShapeDtypeStruct((m, d), BF16)],
        scratch_shapes=[pltpu.VMEM((bm, xd), BF16)],
        compiler_params=_params("parallel"),
        name="xattn_block",
    )(x, g.reshape(1, d).astype(F32), wq, kv_mem, kv_mem, wo, g_next.reshape(1, d).astype(F32))


def _rot_cols(w):
    half = w.shape[-1] // 2
    return jnp.concatenate([-w[..., half:], w[..., :half]], axis=-1)


def _rot_rows(w_t):
    half = w_t.shape[-2] // 2
    return jnp.concatenate([-w_t[..., half:, :], w_t[..., :half, :]], axis=-2)


A_IN_W = (A_Q_HEADS + 2 * A_KV_HEADS) * HEAD_DIM
B_IN_W = 3 * B_HEADS * HEAD_DIM


def _prep_weights(w_in, c_w_q_b, c_w_kv_b, w_out, x_w_q, x_w_k, x_w_v, x_w_o, w_gate, w_up, w_down, ff_pad):
    nl = w_in.shape[0]
    c_off = A_IN_W + B_IN_W
    w_in_t = jnp.swapaxes(w_in, 1, 2)
    w_ckr = w_in_t[:, c_off + C_Q_RANK + C_KV_RANK:]
    w_c_t = jnp.concatenate([w_in_t[:, c_off:], _rot_rows(w_ckr)], axis=1)
    w_c = cast_transposed(w_c_t, w_c_t.shape[1])
    w_a = cast_transposed(w_in_t, A_IN_W)
    w_b = cast_transposed(w_in_t, B_IN_W, row0=A_IN_W)

    wq = c_w_q_b.reshape(nl, C_Q_RANK, C_HEADS, C_NOPE + C_ROPE)
    wq = jnp.concatenate([wq, _rot_cols(wq[..., C_NOPE:])], axis=-1)
    wq_t = wq.reshape(nl, C_Q_RANK, C_HEADS * C_QW).astype(BF16).transpose(0, 2, 1)

    wkv = c_w_kv_b.reshape(nl, C_KV_RANK, C_HEADS, C_NOPE + C_V).astype(BF16)
    wk = wkv[..., :C_NOPE].reshape(nl, C_KV_RANK, C_HEADS * C_NOPE)
    wv_t = wkv[..., C_NOPE:].reshape(nl, C_KV_RANK, C_HEADS * C_V).transpose(0, 2, 1)

    return dict(w_a=w_a, w_b=w_b, w_c=w_c, wq_t=wq_t, wk=wk, wv_t=wv_t, w_out=w_out.astype(BF16),
                wg=cast_pad_cols(w_gate, ff_pad), wu=cast_pad_cols(w_up, ff_pad), wd=cast_pad_rows(w_down, ff_pad),
                x_w_q=x_w_q.astype(BF16), w_xkv=jnp.concatenate([x_w_k, x_w_v], axis=2).astype(BF16),
                x_w_o=x_w_o.astype(BF16))


def _rope_table(seq):
    inv = 1.0 / (ROPE_BASE ** (jnp.arange(0, C_ROPE, 2, dtype=F32) / C_ROPE))
    ang = jnp.arange(seq, dtype=F32)[:, None] * inv[None, :]
    cos, sin = jnp.cos(ang), jnp.sin(ang)
    return jnp.concatenate([cos, cos, sin, sin], axis=1)


def kernel(x, mem, ln_mix, w_in, a_sink, b_rpb, c_q_norm, c_w_q_b, c_kv_norm, c_w_kv_b, w_out,
           ln_xattn, ln_mem, x_w_q, x_w_k, x_w_v, x_w_o, ln_ffn, w_gate, w_up, w_down, ln_final):
    batch, seq, d = x.shape
    n_mem = mem.shape[1]
    depth = w_in.shape[0]
    ff = w_gate.shape[2]
    ff_pad = -(-ff // 1024) * 1024
    table = _rope_table(seq)
    table_t = table.T
    kv_tile = min(C_KV_TILE, seq)
    xs = x.reshape(batch * seq, d)
    mem2 = mem.reshape(batch * n_mem, d)
    w = _prep_weights(w_in, c_w_q_b, c_w_kv_b, w_out, x_w_q, x_w_k, x_w_v, x_w_o, w_gate, w_up, w_down, ff_pad)
    b_bias = _mixer_b_bias(b_rpb, min(B_WIN_ROWS, seq // GRID_W))
    for l in range(depth):
        h, c_all = norm_proj(xs, ln_mix[l], w["w_c"], l)
        qkv_a = matmul(h, w["w_a"], l, BF16, bn=1280, name="w_in_a")
        qkv_b = matmul(h, w["w_b"], l, BF16, bn=1536, name="w_in_b")
        ya = mixer_a(qkv_a, a_sink[l], batch, seq)
        yb = mixer_b(qkv_b, b_bias[l], batch, seq)
        q_t = mla_q_proj(c_all, c_q_norm[l], w["wq_t"], l, table_t, seq)
        kn_c, v_t, kr_c = mla_kv_proj(c_all, c_kv_norm[l], w["wk"], w["wv_t"], l, table, seq, kv_tile)
        yc = mla_attention(q_t, kn_c, kr_c, v_t, batch, seq, kv_tile)
        xs = matmul([ya, yb, yc], w["w_out"], l, F32, res=xs, name="w_out")
        mem_n = rmsnorm(mem2, ln_mem[l], BF16)
        kv_mem = matmul(mem_n, w["w_xkv"], l, BF16, name="xattn_kv")
        xs, h = xattn_block(xs, ln_xattn[l], w["x_w_q"], kv_mem, w["x_w_o"], ln_ffn[l], l, seq, n_mem)
        act = gate_up(h, w["wg"], w["wu"], l)
        xs = matmul(act, w["wd"], l, F32, res=xs, bk=ff_pad // 4, name="ffn_down")
    out = rmsnorm(xs, ln_final, F32)
    return out.reshape(batch, seq, d)
```

```python
import functools
import math

import numpy as np
import jax
import jax.numpy as jnp
from jax import lax
from jax.experimental import pallas as pl
from jax.experimental.pallas import tpu as pltpu

F32 = jnp.float32
BF16 = jnp.bfloat16

HEAD_DIM = 128
GRID_W = 64
A_Q_HEADS = 12
A_KV_HEADS = 4
A_REP = A_Q_HEADS // A_KV_HEADS
A_WINDOW = 128
A_BLOCK = 128
B_HEADS = 8
B_WIN_ROWS = 8
B_WIN_COLS = 16
B_ROW_GROUP = 64
C_HEADS = 12
C_Q_RANK = 1024
C_KV_RANK = 512
C_NOPE = 128
C_ROPE = 64
C_V = 128
ROPE_BASE = 10000.0
X_HEADS = 4
NORM_EPS = 1e-6
NEG_INF = -1e30
LOG2E = 1.4426950408889634

VMEM_LIMIT_BYTES = 58 * 1024 * 1024
LANES = 128


def _params(*sem):
    return pltpu.CompilerParams(dimension_semantics=sem, vmem_limit_bytes=VMEM_LIMIT_BYTES)


def _tile(n, pref):
    if n <= pref:
        return n
    t = (pref // LANES) * LANES
    while t >= LANES:
        if n % t == 0:
            return t
        t -= LANES
    return n


def _rmsnorm_kernel(x_ref, g_ref, o_ref):
    x = x_ref[...].astype(F32)
    inv = lax.rsqrt(jnp.mean(x * x, axis=-1, keepdims=True) + NORM_EPS)
    o_ref[...] = (x * inv * g_ref[...]).astype(o_ref.dtype)


def rmsnorm(x, g, out_dtype, bm=256):
    m, d = x.shape
    bm = min(bm, m)
    return pl.pallas_call(
        _rmsnorm_kernel,
        grid=(m // bm,),
        in_specs=[pl.BlockSpec((bm, d), lambda i: (i, 0)), pl.BlockSpec((1, d), lambda i: (0, 0))],
        out_specs=pl.BlockSpec((bm, d), lambda i: (i, 0)),
        out_shape=jax.ShapeDtypeStruct((m, d), out_dtype),
        compiler_params=_params("parallel"),
        name="rmsnorm",
    )(x, g.reshape(1, d).astype(F32))


def _cast_pad_cols_kernel(x_ref, o_ref):
    n = x_ref.shape[-1]
    o_ref[:, :n] = x_ref[...].astype(o_ref.dtype)
    if o_ref.shape[1] > n:
        o_ref[:, n:] = jnp.zeros((o_ref.shape[0], o_ref.shape[1] - n), o_ref.dtype)


def cast_pad_cols(w, n_pad, br=256):
    nl, r, n = w.shape
    br = _tile(r, br)
    return pl.pallas_call(
        _cast_pad_cols_kernel,
        grid=(nl, r // br),
        in_specs=[pl.BlockSpec((None, br, n), lambda l, i: (l, i, 0))],
        out_specs=pl.BlockSpec((None, br, n_pad), lambda l, i: (l, i, 0)),
        out_shape=jax.ShapeDtypeStruct((nl, r, n_pad), BF16),
        compiler_params=_params("parallel", "parallel"),
        name="cast_pad_cols",
    )(w)


def _cast_pad_rows_kernel(x_ref, o_ref, *, n_src_blocks):
    i = pl.program_id(1)

    @pl.when(i < n_src_blocks)
    def _():
        o_ref[...] = x_ref[...].astype(o_ref.dtype)

    @pl.when(i >= n_src_blocks)
    def _():
        o_ref[...] = jnp.zeros(o_ref.shape, o_ref.dtype)


def cast_pad_rows(w, r_pad, br=256):
    nl, r, n = w.shape
    br = math.gcd(math.gcd(r, r_pad), br)
    nsrc = r // br
    return pl.pallas_call(
        functools.partial(_cast_pad_rows_kernel, n_src_blocks=nsrc),
        grid=(nl, r_pad // br),
        in_specs=[pl.BlockSpec((None, br, n), lambda l, i: (l, jnp.minimum(i, nsrc - 1), 0))],
        out_specs=pl.BlockSpec((None, br, n), lambda l, i: (l, i, 0)),
        out_shape=jax.ShapeDtypeStruct((nl, r_pad, n), BF16),
        compiler_params=_params("parallel", "parallel"),
        name="cast_pad_rows",
    )(w)


def _cast_transposed_kernel(x_ref, o_ref):
    o_ref[...] = x_ref[...].T.astype(o_ref.dtype)


def cast_transposed(w_t, n_rows, row0=0, br=512):
    nl, _, k = w_t.shape
    br = _tile(math.gcd(n_rows, row0) if row0 else n_rows, br)
    rb0 = row0 // br
    return pl.pallas_call(
        _cast_transposed_kernel,
        grid=(nl, n_rows // br),
        in_specs=[pl.BlockSpec((None, br, k), lambda l, j: (l, rb0 + j, 0))],
        out_specs=pl.BlockSpec((None, k, br), lambda l, j: (l, 0, j)),
        out_shape=jax.ShapeDtypeStruct((nl, k, n_rows), BF16),
        compiler_params=_params("parallel", "parallel"),
        name="cast_transposed",
    )(w_t)


def _mm_kernel(*refs, n_a, nk, has_res):
    a_refs = refs[:n_a]
    w_ref = refs[n_a]
    pos = n_a + 1
    r_ref = refs[pos] if has_res else None
    pos += int(has_res)
    o_ref = refs[pos]

    if nk > 1:
        @pl.when(pl.program_id(2) == 0)
        def _():
            o_ref[...] = r_ref[...] if has_res else jnp.zeros(o_ref.shape, o_ref.dtype)

    a = a_refs[0][...] if n_a == 1 else jnp.concatenate([r[...] for r in a_refs], axis=1)
    part = jnp.dot(a, w_ref[...], preferred_element_type=F32)
    if nk > 1:
        o_ref[...] += part
    else:
        if has_res:
            part = r_ref[...] + part
        o_ref[...] = part.astype(o_ref.dtype)


def matmul(a_list, w, layer, out_dtype, *, n=None, col0=0, res=None, bm=1024, bn=1024, bk=None, name="matmul"):
    if not isinstance(a_list, (list, tuple)):
        a_list = [a_list]
    m = a_list[0].shape[0]
    kdim = sum(a.shape[1] for a in a_list)
    assert w.shape[1] == kdim
    n = w.shape[2] - col0 if n is None else n
    bm = _tile(m, bm)
    bn = _tile(math.gcd(n, col0) if col0 else n, bn)
    assert n % bn == 0 and col0 % bn == 0
    n_a = len(a_list)
    if bk is None or n_a > 1:
        nk = 1
    else:
        assert kdim % bk == 0
        nk = kdim // bk
    in_specs = []
    for a in a_list:
        kd = a.shape[1] if nk == 1 else bk
        in_specs.append(pl.BlockSpec((bm, kd), lambda i, j, k: (i, k)))
    cb0 = col0 // bn
    in_specs.append(pl.BlockSpec((None, kdim if nk == 1 else bk, bn), lambda i, j, k: (layer, k, cb0 + j)))
    args = list(a_list) + [w]
    if res is not None:
        in_specs.append(pl.BlockSpec((bm, bn), lambda i, j, k: (i, j)))
        args.append(res)
    assert nk == 1 or out_dtype == F32
    return pl.pallas_call(
        functools.partial(_mm_kernel, n_a=n_a, nk=nk, has_res=res is not None),
        grid=(m // bm, n // bn, nk),
        in_specs=in_specs,
        out_specs=pl.BlockSpec((bm, bn), lambda i, j, k: (i, j)),
        out_shape=jax.ShapeDtypeStruct((m, n), out_dtype),
        compiler_params=_params("parallel", "parallel", "arbitrary"),
        name=name,
    )(*args)


def _norm_proj_kernel(x_ref, g_ref, w_ref, h_ref, o_ref):
    x = x_ref[...]
    inv = lax.rsqrt(jnp.mean(x * x, axis=-1, keepdims=True) + NORM_EPS)
    h = (x * inv * g_ref[...]).astype(h_ref.dtype)
    h_ref[...] = h
    o_ref[...] = jnp.dot(h, w_ref[...], preferred_element_type=F32)


def norm_proj(x, g, w, layer, bm=512):
    m, d = x.shape
    n = w.shape[2]
    bm = _tile(m, bm)
    return pl.pallas_call(
        _norm_proj_kernel,
        grid=(m // bm,),
        in_specs=[pl.BlockSpec((bm, d), lambda i: (i, 0)),
                  pl.BlockSpec((1, d), lambda i: (0, 0)),
                  pl.BlockSpec((None, d, n), lambda i: (layer, 0, 0), pipeline_mode=pl.Buffered(1))],
        out_specs=[pl.BlockSpec((bm, d), lambda i: (i, 0)), pl.BlockSpec((bm, n), lambda i: (i, 0))],
        out_shape=[jax.ShapeDtypeStruct((m, d), BF16), jax.ShapeDtypeStruct((m, n), F32)],
        compiler_params=_params("parallel"),
        name="norm_proj",
    )(x, g.reshape(1, d).astype(F32), w)


def _gate_up_kernel(h_ref, wg_ref, wu_ref, o_ref):
    h = h_ref[...]
    g = jnp.dot(h, wg_ref[...], preferred_element_type=F32)
    u = jnp.dot(h, wu_ref[...], preferred_element_type=F32)
    o_ref[...] = (jax.nn.silu(g) * u).astype(o_ref.dtype)


def gate_up(h, wg, wu, layer, bm=1024, bn=512):
    m, d = h.shape
    n = wg.shape[2]
    bm = _tile(m, bm)
    bn = _tile(n, bn)
    return pl.pallas_call(
        _gate_up_kernel,
        grid=(m // bm, n // bn),
        in_specs=[
            pl.BlockSpec((bm, d), lambda i, j: (i, 0)),
            pl.BlockSpec((None, d, bn), lambda i, j: (layer, 0, j)),
            pl.BlockSpec((None, d, bn), lambda i, j: (layer, 0, j)),
        ],
        out_specs=pl.BlockSpec((bm, bn), lambda i, j: (i, j)),
        out_shape=jax.ShapeDtypeStruct((m, n), BF16),
        compiler_params=_params("parallel", "parallel"),
        name="gate_up",
    )(h, wg, wu)


def _ffn_fused_kernel(h_ref, wg_ref, wu_ref, wd_ref, r_ref, o_ref):
    @pl.when(pl.program_id(1) == 0)
    def _():
        o_ref[...] = r_ref[...]

    h = h_ref[...]
    g = jnp.dot(h, wg_ref[...], preferred_element_type=F32)
    u = jnp.dot(h, wu_ref[...], preferred_element_type=F32)
    act = (jax.nn.silu(g) * u).astype(BF16)
    o_ref[...] += jnp.dot(act, wd_ref[...], preferred_element_type=F32)


def ffn_fused(h, wg, wu, wd, layer, res, bm=512, tf=256):
    m, d = h.shape
    ff = wg.shape[2]
    bm = _tile(m, bm)
    tf = _tile(ff, tf)
    return pl.pallas_call(
        _ffn_fused_kernel,
        grid=(m // bm, ff // tf),
        in_specs=[
            pl.BlockSpec((bm, d), lambda i, t: (i, 0)),
            pl.BlockSpec((None, d, tf), lambda i, t: (layer, 0, t)),
            pl.BlockSpec((None, d, tf), lambda i, t: (layer, 0, t)),
            pl.BlockSpec((None, tf, d), lambda i, t: (layer, t, 0)),
            pl.BlockSpec((bm, d), lambda i, t: (i, 0), pipeline_mode=pl.Buffered(1)),
        ],
        out_specs=pl.BlockSpec((bm, d), lambda i, t: (i, 0)),
        out_shape=jax.ShapeDtypeStruct((m, d), F32),
        compiler_params=_params("parallel", "arbitrary"),
        name="ffn_fused",
    )(h, wg, wu, wd, res)


def _mixer_a_kernel(slope_ref, sink_ref, q_ref, kp_ref, kc_ref, kn_ref, vp_ref, vc_ref, vn_ref, o_ref,
                    *, tq, seq):
    i = pl.program_id(1)
    g = pl.program_id(2)
    nsub = tq // A_BLOCK
    kcat = jnp.concatenate([kp_ref[...], kc_ref[...], kn_ref[...]], axis=0)
    vcat = jnp.concatenate([vp_ref[...], vc_ref[...], vn_ref[...]], axis=0)
    qi = lax.broadcasted_iota(jnp.int32, (A_BLOCK, 3 * A_BLOCK), 0)
    kj = lax.broadcasted_iota(jnp.int32, (A_BLOCK, 3 * A_BLOCK), 1)
    rel = kj - A_BLOCK - qi
    dist = jnp.abs(rel)
    dist_f = dist.astype(F32)
    in_window = dist <= A_WINDOW
    scale = 1.0 / math.sqrt(HEAD_DIM)
    units = [(j, r) for j in range(nsub) for r in range(A_REP)]
    scores = []
    for j, r in units:
        kband = kcat[j * A_BLOCK:(j + 3) * A_BLOCK]
        q = q_ref[j * A_BLOCK:(j + 1) * A_BLOCK, r * HEAD_DIM:(r + 1) * HEAD_DIM]
        scores.append(lax.dot_general(q, kband, (((1,), (1,)), ((), ())), preferred_element_type=F32))
    alibi = [jnp.where(in_window, -(slope_ref[g * A_REP + r] * LOG2E) * dist_f, NEG_INF) for r in range(A_REP)]
    in_seq = {}
    for j in {0, nsub - 1}:
        kpos = (i * nsub + j - 1) * A_BLOCK + kj
        in_seq[j] = (kpos >= 0) & (kpos < seq)
    probs, inv_denoms = [], []
    for (j, r), s in zip(units, scores):
        sink = sink_ref[g * A_REP + r] * LOG2E
        s = s * (scale * LOG2E) + alibi[r]
        if j in in_seq:
            s = jnp.where(in_seq[j], s, NEG_INF)
        mx = jnp.maximum(jnp.max(s, axis=-1, keepdims=True), sink)
        p = jnp.exp2(s - mx)
        denom = jnp.sum(p, axis=-1, keepdims=True) + jnp.exp2(sink - mx)
        probs.append(p.astype(BF16))
        inv_denoms.append(1.0 / denom)
    for (j, r), p, inv in zip(units, probs, inv_denoms):
        vband = vcat[j * A_BLOCK:(j + 3) * A_BLOCK]
        o = jnp.dot(p, vband, preferred_element_type=F32) * inv
        o_ref[j * A_BLOCK:(j + 1) * A_BLOCK, r * HEAD_DIM:(r + 1) * HEAD_DIM] = o.astype(o_ref.dtype)


def mixer_a(qkv, sink, batch, seq, tq=2048):
    tq = min(tq, seq)
    nq = seq // tq
    nsub = tq // A_BLOCK
    nblk = seq // A_BLOCK
    kcol = A_Q_HEADS
    vcol = A_Q_HEADS + A_KV_HEADS
    slopes = 2.0 ** (-8.0 * jnp.arange(1, A_Q_HEADS + 1, dtype=F32) / A_Q_HEADS)

    def cur(col):
        return pl.BlockSpec((tq, HEAD_DIM), lambda b, i, g: (b * nq + i, col + g))

    def prev(col):
        return pl.BlockSpec((A_BLOCK, HEAD_DIM),
                            lambda b, i, g: (b * nblk + jnp.maximum(i * nsub - 1, 0), col + g))

    def nxt(col):
        return pl.BlockSpec((A_BLOCK, HEAD_DIM),
                            lambda b, i, g: (b * nblk + jnp.minimum((i + 1) * nsub, nblk - 1), col + g))

    smem = pl.BlockSpec(memory_space=pltpu.SMEM)
    return pl.pallas_call(
        functools.partial(_mixer_a_kernel, tq=tq, seq=seq),
        grid=(batch, nq, A_KV_HEADS),
        in_specs=[smem, smem,
                  pl.BlockSpec((tq, A_REP * HEAD_DIM), lambda b, i, g: (b * nq + i, g)),
                  prev(kcol), cur(kcol), nxt(kcol), prev(vcol), cur(vcol), nxt(vcol)],
        out_specs=pl.BlockSpec((tq, A_REP * HEAD_DIM), lambda b, i, g: (b * nq + i, g)),
        out_shape=jax.ShapeDtypeStruct((batch * seq, A_Q_HEADS * HEAD_DIM), BF16),
        compiler_params=_params("parallel", "parallel", "parallel"),
        name="mixer_a",
    )(slopes, sink.astype(F32), qkv, qkv, qkv, qkv, qkv, qkv, qkv)


def _mixer_b_kernel(q_ref, k_ref, v_ref, bias_ref, o_ref, *, rows, wr, group):
    scale = 1.0 / math.sqrt(HEAD_DIM)
    nkeys = wr * GRID_W

    def row_slice(r, n):
        return pl.ds(pl.multiple_of(r * GRID_W, GRID_W), n)

    def body(gi, carry):
        rws = [gi * group + j for j in range(group)]
        starts = [jnp.clip(r - wr // 2, 0, rows - wr) for r in rws]
        scores = [lax.dot_general(q_ref[row_slice(r, GRID_W), :], k_ref[row_slice(rs, nkeys), :],
                                  (((1,), (1,)), ((), ())), preferred_element_type=F32)
                  for r, rs in zip(rws, starts)]
        probs, inv_denoms = [], []
        for r, rs, s in zip(rws, starts, scores):
            s = s * (scale * LOG2E) + bias_ref[0, r - rs]
            mx = jnp.max(s, axis=-1, keepdims=True)
            p = jnp.exp2(s - mx)
            probs.append(p.astype(BF16))
            inv_denoms.append(1.0 / jnp.sum(p, axis=-1, keepdims=True))
        for r, rs, p, inv in zip(rws, starts, probs, inv_denoms):
            o = jnp.dot(p, v_ref[row_slice(rs, nkeys), :], preferred_element_type=F32) * inv
            o_ref[row_slice(r, GRID_W), :] = o.astype(o_ref.dtype)
        return carry

    lax.fori_loop(0, rows // group, body, 0)


def _mixer_b_bias(rpb, wr):
    qc = np.arange(GRID_W)[:, None]
    kc = np.arange(GRID_W)[None, :]
    cstart = np.clip(qc - B_WIN_COLS // 2, 0, GRID_W - B_WIN_COLS)
    col_ok = (kc >= cstart) & (kc < cstart + B_WIN_COLS)
    lo = GRID_W - B_WIN_COLS
    tbl = jnp.pad(rpb.astype(F32), ((0, 0), (0, 0), (0, 0), (lo, lo)), mode="edge")
    b = jnp.stack([tbl[..., GRID_W - 1 - q:2 * GRID_W - 1 - q] for q in range(GRID_W)], axis=-2)
    b = jnp.stack([b[:, :, B_WIN_ROWS - 1 - v:B_WIN_ROWS - 1 - v + wr] for v in range(wr)], axis=2)
    b = jnp.where(jnp.asarray(col_ok), b * LOG2E, NEG_INF)
    b = b.transpose(0, 1, 2, 4, 3, 5)
    return b.reshape(rpb.shape[0], rpb.shape[1], wr, GRID_W, wr * GRID_W)


def mixer_b(qkv, bias, batch, seq):
    rows = seq // GRID_W
    wr = min(B_WIN_ROWS, rows)
    group = math.gcd(rows, B_ROW_GROUP)

    def col(off):
        return pl.BlockSpec((seq, HEAD_DIM), lambda b, h: (b, off + h))

    return pl.pallas_call(
        functools.partial(_mixer_b_kernel, rows=rows, wr=wr, group=group),
        grid=(batch, B_HEADS),
        in_specs=[col(0), col(B_HEADS), col(2 * B_HEADS),
                  pl.BlockSpec((1, wr, GRID_W, wr * GRID_W), lambda b, h: (h, 0, 0, 0))],
        out_specs=pl.BlockSpec((seq, HEAD_DIM), lambda b, h: (b, h)),
        out_shape=jax.ShapeDtypeStruct((batch * seq, B_HEADS * HEAD_DIM), BF16),
        compiler_params=_params("parallel", "parallel"),
        name="mixer_b",
    )(qkv, qkv, qkv, bias)


C_QW = 2 * HEAD_DIM
C_ONES_ROWS = 16
C_KV_TILE = 2048
C_LOOKAHEAD = 2
_NT = (((1,), (1,)), ((), ()))


def _rms_bf16(c, g):
    inv = lax.rsqrt(jnp.mean(c * c, axis=-1, keepdims=True) + NORM_EPS)
    return (c * inv * g).astype(BF16)


def _mla_q_kernel(c_ref, g_ref, w_ref, t_ref, o_ref, *, scale):
    a = _rms_bf16(c_ref[...], g_ref[...])
    table = t_ref[...]
    for h in range(C_HEADS):
        r0 = h * C_QW
        acc = lax.dot_general(w_ref[r0:r0 + C_QW, :], a, _NT, preferred_element_type=F32)
        o_ref[r0:r0 + C_NOPE, :] = (acc[:C_NOPE] * scale).astype(o_ref.dtype)
        t = acc[C_NOPE:] * table
        rope = (t[:C_ROPE] + t[C_ROPE:]) * scale
        o_ref[r0 + C_NOPE:r0 + C_NOPE + C_ROPE, :] = rope.astype(o_ref.dtype)
        o_ref[r0 + C_NOPE + C_ROPE:r0 + C_QW, :] = jnp.zeros((C_ROPE, rope.shape[1]), o_ref.dtype)


def mla_q_proj(c_all, q_norm, wq_t, layer, table_t, seq, bm=512):
    m = c_all.shape[0]
    bm = _tile(min(m, seq), bm)
    nt = seq // bm
    scale = LOG2E / math.sqrt(C_NOPE + C_ROPE)
    return pl.pallas_call(
        functools.partial(_mla_q_kernel, scale=scale),
        grid=(m // bm,),
        in_specs=[pl.BlockSpec((bm, C_Q_RANK), lambda i: (i, 0)),
                  pl.BlockSpec((1, C_Q_RANK), lambda i: (0, 0)),
                  pl.BlockSpec((None, C_HEADS * C_QW, C_Q_RANK), lambda i: (layer, 0, 0)),
                  pl.BlockSpec((LANES, bm), lambda i: (0, i % nt))],
        out_specs=pl.BlockSpec((C_HEADS * C_QW, bm), lambda i: (0, i)),
        out_shape=jax.ShapeDtypeStruct((C_HEADS * C_QW, m), BF16),
        compiler_params=_params("parallel"),
        name="mla_q_proj",
    )(c_all, q_norm.reshape(1, C_Q_RANK).astype(F32), wq_t, table_t)


def _mla_kv_kernel(c_ref, g_ref, wk_ref, wvt_ref, kr_ref, t_ref, ok_ref, ovt_ref, okr_ref):
    a = _rms_bf16(c_ref[...], g_ref[...])
    ok_ref[...] = jnp.dot(a, wk_ref[...], preferred_element_type=F32).astype(ok_ref.dtype)
    ovt_ref[0] = lax.dot_general(wvt_ref[...], a, _NT, preferred_element_type=F32).astype(ovt_ref.dtype)
    t = kr_ref[...] * t_ref[...]
    rope = t + pltpu.roll(t, C_ROPE, axis=1)
    lane = lax.broadcasted_iota(jnp.int32, rope.shape, 1)
    okr_ref[...] = jnp.where(lane < C_ROPE, rope, 0.0).astype(okr_ref.dtype)


def mla_kv_proj(c_all, kv_norm, wk, wv_t, layer, table, seq, bm):
    m = c_all.shape[0]
    nt = seq // bm
    n = wk.shape[2]
    ckv_blk = C_Q_RANK // C_KV_RANK
    kr_blk = (C_Q_RANK + C_KV_RANK) // LANES
    return pl.pallas_call(
        _mla_kv_kernel,
        grid=(m // bm,),
        in_specs=[pl.BlockSpec((bm, C_KV_RANK), lambda i: (i, ckv_blk)),
                  pl.BlockSpec((1, C_KV_RANK), lambda i: (0, 0)),
                  pl.BlockSpec((None, C_KV_RANK, n), lambda i: (layer, 0, 0)),
                  pl.BlockSpec((None, n, C_KV_RANK), lambda i: (layer, 0, 0)),
                  pl.BlockSpec((bm, LANES), lambda i: (i, kr_blk)),
                  pl.BlockSpec((bm, LANES), lambda i: (i % nt, 0))],
        out_specs=[pl.BlockSpec((bm, n), lambda i: (i, 0)),
                   pl.BlockSpec((1, n, bm), lambda i: (i, 0, 0)),
                   pl.BlockSpec((bm, LANES), lambda i: (i, 0))],
        out_shape=[jax.ShapeDtypeStruct((m, n), BF16),
                   jax.ShapeDtypeStruct((m // bm, n, bm), BF16),
                   jax.ShapeDtypeStruct((m, LANES), BF16)],
        compiler_params=_params("parallel"),
        name="mla_kv_proj",
    )(c_all, kv_norm.reshape(1, C_KV_RANK).astype(F32), wk, wv_t, c_all, table)


def _mla_attn_kernel(qt_ref, kn_ref, kr_ref, vt_ref, o_ref, s_ref, m_ref, acc_ref, *, bk, nkv, cw):
    bq = qt_ref.shape[1]
    cols = [slice(c * cw, (c + 1) * cw) for c in range(bq // cw)]
    m_ref[...] = jnp.full(m_ref.shape, NEG_INF, F32)
    acc_ref[...] = jnp.zeros(acc_ref.shape, F32)
    ones = jnp.ones((C_ONES_ROWS, bk), BF16)

    def keys(t):
        off = pl.multiple_of(t * bk, bk)
        return jnp.concatenate([kn_ref[pl.ds(off, bk), :], kr_ref[pl.ds(off, bk), :]], axis=1)

    ncb = len(cols)

    def scores(k, c):
        return jnp.dot(k, qt_ref[:, cols[c]], preferred_element_type=F32)

    def softmax_update(slot, cs):
        s = s_ref[slot]
        m_old = m_ref[:, cs]
        m_new = jnp.maximum(m_old, jnp.max(s, axis=0, keepdims=True))
        m_ref[:, cs] = m_new
        return jnp.exp2(s - m_new).astype(BF16), jnp.exp2(m_old - m_new)

    def step(t, last):
        vt = jnp.concatenate([vt_ref[t], ones], axis=0)
        k_cur = keys(t)
        k_nxt = None if last else keys(t + 1)
        for c, cs in enumerate(cols):
            p, alpha = softmax_update(c, cs)
            ahead = c + C_LOOKAHEAD
            if ahead < ncb:
                s_ref[ahead] = scores(k_cur, ahead)
            elif not last:
                s_ref[ahead - ncb] = scores(k_nxt, ahead - ncb)
            acc_ref[:, cs] = alpha * acc_ref[:, cs] + jnp.dot(vt, p, preferred_element_type=F32)

    k0 = keys(0)
    for c in range(C_LOOKAHEAD):
        s_ref[c] = scores(k0, c)

    def body(t, carry):
        step(t, False)
        return carry

    lax.fori_loop(0, nkv - 1, body, 0)
    step(nkv - 1, True)
    out_t = acc_ref[:C_V, :] / acc_ref[C_V:C_V + 1, :]
    o_ref[...] = out_t.T.astype(o_ref.dtype)


def mla_attention(q_t, kn, kr, v_t, batch, seq, bk, bq=2048, cw=256):
    bq = min(bq, seq)
    cw = min(cw, bq)
    nq = seq // bq
    nkv = seq // bk
    return pl.pallas_call(
        functools.partial(_mla_attn_kernel, bk=bk, nkv=nkv, cw=cw),
        grid=(batch, C_HEADS, nq),
        in_specs=[pl.BlockSpec((C_QW, bq), lambda b, h, i: (h, b * nq + i)),
                  pl.BlockSpec((seq, C_NOPE), lambda b, h, i: (b, h)),
                  pl.BlockSpec((seq, LANES), lambda b, h, i: (b, 0)),
                  pl.BlockSpec((nkv, C_V, bk), lambda b, h, i: (b, h, 0))],
        out_specs=pl.BlockSpec((bq, C_V), lambda b, h, i: (b * nq + i, h)),
        out_shape=jax.ShapeDtypeStruct((batch * seq, C_HEADS * C_V), BF16),
        scratch_shapes=[pltpu.VMEM((bq // cw, bk, cw), F32), pltpu.VMEM((1, bq), F32),
                        pltpu.VMEM((C_V + C_ONES_ROWS, bq), F32)],
        compiler_params=_params("parallel", "parallel", "arbitrary"),
        name="mla_attention",
    )(q_t, kn, kr, v_t)


def _xattn_kernel(x_ref, g_ref, wq_ref, k_ref, v_ref, wo_ref, g2_ref, xo_ref, ho_ref, o_scr):
    x = x_ref[...]
    h = _rms_bf16(x, g_ref[...])
    q = jnp.dot(h, wq_ref[...], preferred_element_type=F32).astype(BF16)
    scale = 1.0 / math.sqrt(HEAD_DIM)
    heads = [slice(hh * HEAD_DIM, (hh + 1) * HEAD_DIM) for hh in range(X_HEADS)]
    scores = [lax.dot_general(q[:, sl], k_ref[:, sl], (((1,), (1,)), ((), ())), preferred_element_type=F32)
              for sl in heads]
    probs, inv_denoms = [], []
    for s in scores:
        s = s * (scale * LOG2E)
        mx = jnp.max(s, axis=-1, keepdims=True)
        p = jnp.exp2(s - mx)
        probs.append(p.astype(BF16))
        inv_denoms.append(1.0 / jnp.sum(p, axis=-1, keepdims=True))
    for sl, p, inv in zip(heads, probs, inv_denoms):
        o = jnp.dot(p, v_ref[:, sl], preferred_element_type=F32) * inv
        o_scr[:, sl] = o.astype(o_scr.dtype)
    x_new = x + jnp.dot(o_scr[...], wo_ref[...], preferred_element_type=F32)
    xo_ref[...] = x_new
    ho_ref[...] = _rms_bf16(x_new, g2_ref[...])


def xattn_block(x, g, wq, kv_mem, wo, g_next, layer, seq, n_mem, bm=512):
    m, d = x.shape
    bm = _tile(min(m, seq), bm)
    per_b = seq // bm
    xd = X_HEADS * HEAD_DIM
    once = pl.Buffered(1)
    return pl.pallas_call(
        _xattn_kernel,
        grid=(m // bm,),
        in_specs=[pl.BlockSpec((bm, d), lambda i: (i, 0)),
                  pl.BlockSpec((1, d), lambda i: (0, 0)),
                  pl.BlockSpec((None, d, xd), lambda i: (layer, 0, 0), pipeline_mode=once),
                  pl.BlockSpec((n_mem, xd), lambda i: (i // per_b, 0)),
                  pl.BlockSpec((n_mem, xd), lambda i: (i // per_b, 1)),
                  pl.BlockSpec((None, xd, d), lambda i: (layer, 0, 0), pipeline_mode=once),
                  pl.BlockSpec((1, d), lambda i: (0, 0))],
        out_specs=[pl.BlockSpec((bm, d), lambda i: (i, 0)), pl.BlockSpec((bm, d), lambda i: (i, 0))],
        out_shape=[jax.ShapeDtypeStruct((m, d), F32), jax.ShapeDtypeStruct((m, d), BF16)],
        scratch_shapes=[pltpu.VMEM((bm, xd), BF16)],
        compiler_params=_params("parallel"),
        name="xattn_block",
    )(x, g.reshape(1, d).astype(F32), wq, kv_mem, kv_mem, wo, g_next.reshape(1, d).astype(F32))


def _rot_cols(w):
    half = w.shape[-1] // 2
    return jnp.concatenate([-w[..., half:], w[..., :half]], axis=-1)


def _rot_rows(w_t):
    half = w_t.shape[-2] // 2
    return jnp.concatenate([-w_t[..., half:, :], w_t[..., :half, :]], axis=-2)


A_IN_W = (A_Q_HEADS + 2 * A_KV_HEADS) * HEAD_DIM
B_IN_W = 3 * B_HEADS * HEAD_DIM


def _prep_weights(w_in, c_w_q_b, c_w_kv_b, w_out, x_w_q, x_w_k, x_w_v, x_w_o, w_gate, w_up, w_down, ff_pad):
    nl = w_in.shape[0]
    c_off = A_IN_W + B_IN_W
    w_in_t = jnp.swapaxes(w_in, 1, 2)
    w_ckr = w_in_t[:, c_off + C_Q_RANK + C_KV_RANK:]
    w_c_t = jnp.concatenate([w_in_t[:, c_off:], _rot_rows(w_ckr)], axis=1)
    w_c = cast_transposed(w_c_t, w_c_t.shape[1])
    w_a = cast_transposed(w_in_t, A_IN_W)
    w_b = cast_transposed(w_in_t, B_IN_W, row0=A_IN_W)

    wq = c_w_q_b.reshape(nl, C_Q_RANK, C_HEADS, C_NOPE + C_ROPE)
    wq = jnp.concatenate([wq, _rot_cols(wq[..., C_NOPE:])], axis=-1)
    wq_t = wq.reshape(nl, C_Q_RANK, C_HEADS * C_QW).astype(BF16).transpose(0, 2, 1)

    wkv = c_w_kv_b.reshape(nl, C_KV_RANK, C_HEADS, C_NOPE + C_V).astype(BF16)
    wk = wkv[..., :C_NOPE].reshape(nl, C_KV_RANK, C_HEADS * C_NOPE)
    wv_t = wkv[..., C_NOPE:].reshape(nl, C_KV_RANK, C_HEADS * C_V).transpose(0, 2, 1)

    return dict(w_a=w_a, w_b=w_b, w_c=w_c, wq_t=wq_t, wk=wk, wv_t=wv_t, w_out=w_out.astype(BF16),
                wg=cast_pad_cols(w_gate, ff_pad), wu=cast_pad_cols(w_up, ff_pad), wd=cast_pad_rows(w_down, ff_pad),
                x_w_q=x_w_q.astype(BF16), w_xkv=jnp.concatenate([x_w_k, x_w_v], axis=2).astype(BF16),
                x_w_o=x_w_o.astype(BF16))


def _rope_table(seq):
    inv = 1.0 / (ROPE_BASE ** (jnp.arange(0, C_ROPE, 2, dtype=F32) / C_ROPE))
    ang = jnp.arange(seq, dtype=F32)[:, None] * inv[None, :]
    cos, sin = jnp.cos(ang), jnp.sin(ang)
    return jnp.concatenate([cos, cos, sin, sin], axis=1)


def kernel(x, mem, ln_mix, w_in, a_sink, b_rpb, c_q_norm, c_w_q_b, c_kv_norm, c_w_kv_b, w_out,
           ln_xattn, ln_mem, x_w_q, x_w_k, x_w_v, x_w_o, ln_ffn, w_gate, w_up, w_down, ln_final):
    batch, seq, d = x.shape
    n_mem = mem.shape[1]
    depth = w_in.shape[0]
    ff = w_gate.shape[2]
    ff_pad = -(-ff // 1024) * 1024
    table = _rope_table(seq)
    table_t = table.T
    kv_tile = min(C_KV_TILE, seq)
    xs = x.reshape(batch * seq, d)
    mem2 = mem.reshape(batch * n_mem, d)
    w = _prep_weights(w_in, c_w_q_b, c_w_kv_b, w_out, x_w_q, x_w_k, x_w_v, x_w_o, w_gate, w_up, w_down, ff_pad)
    b_bias = _mixer_b_bias(b_rpb, min(B_WIN_ROWS, seq // GRID_W))
    for l in range(depth):
        h, c_all = norm_proj(xs, ln_mix[l], w["w_c"], l)
        qkv_a = matmul(h, w["w_a"], l, BF16, bn=1280, name="w_in_a")
        qkv_b = matmul(h, w["w_b"], l, BF16, bn=1536, name="w_in_b")
        ya = mixer_a(qkv_a, a_sink[l], batch, seq)
        yb = mixer_b(qkv_b, b_bias[l], batch, seq)
        q_t = mla_q_proj(c_all, c_q_norm[l], w["wq_t"], l, table_t, seq)
        kn_c, v_t, kr_c = mla_kv_proj(c_all, c_kv_norm[l], w["wk"], w["wv_t"], l, table, seq, kv_tile)
        yc = mla_attention(q_t, kn_c, kr_c, v_t, batch, seq, kv_tile)
        xs = matmul([ya, yb, yc], w["w_out"], l, F32, res=xs, name="w_out")
        mem_n = rmsnorm(mem2, ln_mem[l], BF16)
        kv_mem = matmul(mem_n, w["w_xkv"], l, BF16, name="xattn_kv")
        xs, h = xattn_block(xs, ln_xattn[l], w["x_w_q"], kv_mem, w["x_w_o"], ln_ffn[l], l, seq, n_mem)
        xs = ffn_fused(h, w["wg"], w["wu"], w["wd"], l, xs)
    out = rmsnorm(xs, ln_final, F32)
    return out.reshape(batch, seq, d)
```
